```python
import jax, jax.numpy as jnp
from jax import lax
import numpy as np

D_MODEL = 1024
BATCH = 32
SEQ = 256
DEPTH = 4
DEC_BATCH = 4
DEC_SEQ = 4096
PAST_LEN = 512

GRID_W = 64
N_AB = (DEPTH + 1) // 2
N_C = DEPTH // 2
MLA_HEADS = 8
QK_NOPE_DIM = 64
QK_ROPE_DIM = 32
V_HEAD_DIM = 64
Q_LORA_RANK = 384
KV_LORA_RANK = 256
ROPE_BASE = 10000.0
ROPE_AXIS_DIM = QK_ROPE_DIM // 2
Q_BLOCK = 128
ATTN_SCALE = (QK_NOPE_DIM + QK_ROPE_DIM) ** -0.5
GMLP_GROUPS = 8
GMLP_CHUNK = 128
GMLP_WIDTH = D_MODEL // 2
GMLP_GROUP_DIM = GMLP_WIDTH // GMLP_GROUPS
QA_END = Q_LORA_RANK
KVA_END = QA_END + KV_LORA_RANK
KPE_END = KVA_END + QK_ROPE_DIM
IN_AB = KPE_END + 2 * GMLP_WIDTH
MIX_AB = MLA_HEADS * V_HEAD_DIM + GMLP_WIDTH
POOL_WINDOWS = (2, 4, 8, 16)
POOL_GROUP_DIM = D_MODEL // len(POOL_WINDOWS)
D_FF = 4 * D_MODEL
N_MOD = 6
EPS = 1e-6

kernel_name = 'hybrid_mla_gmlp_pool_diffusion_step'


def rms_norm(x, g):
    x32 = x.astype(jnp.float32)
    y = x32 * lax.rsqrt(jnp.mean(x32 * x32, axis=-1, keepdims=True) + EPS)
    return (y * g.astype(jnp.float32)).astype(x.dtype)


def axial_angles(n_tokens):
    rows = n_tokens // GRID_W
    row = jnp.repeat(jnp.arange(rows), GRID_W).astype(jnp.float32)
    col = jnp.tile(jnp.arange(GRID_W), rows).astype(jnp.float32)
    inv = ROPE_BASE ** (-jnp.arange(0, ROPE_AXIS_DIM, 2, dtype=jnp.float32) / ROPE_AXIS_DIM)
    return jnp.stack([row[:, None] * inv, col[:, None] * inv], axis=1)


def apply_axial_rope(x, ang):
    B, T, H, _ = x.shape
    F = QK_ROPE_DIM // 4
    xr = x.reshape(B, T, H, 2, 2, F)
    x1, x2 = xr[..., 0, :], xr[..., 1, :]
    cos = jnp.cos(ang)[:, None].astype(x.dtype)
    sin = jnp.sin(ang)[:, None].astype(x.dtype)
    out = jnp.stack([x1 * cos - x2 * sin, x2 * cos + x1 * sin], axis=-2)
    return out.reshape(x.shape)


def mla_attention(q_nope, q_pe, k_nope, k_pe, v):
    B, T, H, _ = q_nope.shape
    nb = T // Q_BLOCK
    qn = q_nope.reshape(B, nb, Q_BLOCK, H, QK_NOPE_DIM).swapaxes(0, 1)
    qp = q_pe.reshape(B, nb, Q_BLOCK, H, QK_ROPE_DIM).swapaxes(0, 1)

    def block(args):
        qn_b, qp_b = args
        s = (jnp.einsum('bqhd,bkhd->bhqk', qn_b, k_nope)
             + jnp.einsum('bqhd,bkd->bhqk', qp_b, k_pe))
        p = jax.nn.softmax(s.astype(jnp.float32) * ATTN_SCALE, axis=-1).astype(v.dtype)
        return jnp.einsum('bhqk,bkhd->bqhd', p, v)

    o = lax.map(block, (qn, qp))
    return o.swapaxes(0, 1).reshape(B, T, H * V_HEAD_DIM)


def chunk_gmlp(u, v, v_g, w_s, b_s):
    B, T, _ = v.shape
    vn = rms_norm(v, v_g).reshape(B, T // GMLP_CHUNK, GMLP_CHUNK, GMLP_GROUPS, GMLP_GROUP_DIM)
    s = jnp.einsum('gpq,bnqgc->bnpgc', w_s, vn) + b_s.T[:, :, None]
    return u * s.reshape(B, T, GMLP_WIDTH)


def mixer_ab(h, w_in, q_a_g, kv_a_g, w_q_b, w_kv_b, v_g, w_s, b_s, w_out, ang, ctx_ckv, ctx_kpe):
    B, T, _ = h.shape
    z = h @ w_in
    q_a, kv_a, k_pe = z[..., :QA_END], z[..., QA_END:KVA_END], z[..., KVA_END:KPE_END]
    uv = jax.nn.gelu(z[..., KPE_END:])
    q = (rms_norm(q_a, q_a_g) @ w_q_b).reshape(B, T, MLA_HEADS, QK_NOPE_DIM + QK_ROPE_DIM)
    q_nope, q_pe = q[..., :QK_NOPE_DIM], q[..., QK_NOPE_DIM:]
    c_kv = rms_norm(kv_a, kv_a_g)
    if ang is None:
        keys_ckv, keys_kpe = c_kv, k_pe
    else:
        q_pe = apply_axial_rope(q_pe, ang)
        k_pe_rot = apply_axial_rope(k_pe[:, :, None, :], ang)[:, :, 0, :]
        keys_ckv = jnp.concatenate([ctx_ckv, c_kv], axis=1)
        keys_kpe = jnp.concatenate([ctx_kpe, k_pe_rot], axis=1)
    L = keys_ckv.shape[1]
    kv = (keys_ckv @ w_kv_b).reshape(B, L, MLA_HEADS, QK_NOPE_DIM + V_HEAD_DIM)
    k_nope, v = kv[..., :QK_NOPE_DIM], kv[..., QK_NOPE_DIM:]
    attn = mla_attention(q_nope, q_pe, k_nope, keys_kpe, v)
    gm = chunk_gmlp(uv[..., :GMLP_WIDTH], uv[..., GMLP_WIDTH:], v_g, w_s, b_s)
    out = jnp.concatenate([attn, gm], axis=-1) @ w_out
    return out, c_kv, k_pe


def pool_mixer(h, w_pool, scale):
    B, T, _ = h.shape
    h32 = h.astype(jnp.float32)
    cs = jnp.concatenate([jnp.zeros_like(h32[:, :1]), jnp.cumsum(h32, axis=1)], axis=1)
    t = jnp.arange(T)
    outs = []
    for gi, w in enumerate(POOL_WINDOWS):
        sl = slice(gi * POOL_GROUP_DIM, (gi + 1) * POOL_GROUP_DIM)
        start = jnp.clip(t - w // 2, 0, T)
        end = jnp.clip(t + w // 2, 0, T)
        cnt = (end - start).astype(jnp.float32)[None, :, None]
        mean = (cs[:, end, sl] - cs[:, start, sl]) / cnt
        outs.append((mean - h32[..., sl]).astype(h.dtype) @ w_pool[gi])
    return jnp.concatenate(outs, axis=-1) * scale


def squared_relu_mlp(h, w1, w2):
    a = jax.nn.relu(h @ w1)
    return (a * a) @ w2


def trunk(x, cond, ang, ctx_ckv, ctx_kpe, w_mod, b_mod, norm1_g, norm2_g, w_in_ab, q_a_g, kv_a_g,
          w_q_b, w_kv_b, gmlp_v_g, w_spatial, b_spatial, w_out_ab, w_pool, pool_scale,
          w_ff1, w_ff2, final_g):
    ckv_list, kpe_list = [], []
    for l in range(DEPTH):
        mod = (jax.nn.silu(cond) @ w_mod[l] + b_mod[l]).reshape(-1, 1, N_MOD * D_MODEL)
        sh1, sc1, g1, sh2, sc2, g2 = jnp.split(mod, N_MOD, axis=-1)
        h = rms_norm(x, norm1_g[l]) * (1 + sc1) + sh1
        if l % 2 == 0:
            i = l // 2
            c_ckv = None if ang is None else ctx_ckv[:, i]
            c_kpe = None if ang is None else ctx_kpe[:, i]
            mix, ckv, kpe = mixer_ab(h, w_in_ab[i], q_a_g[i], kv_a_g[i], w_q_b[i], w_kv_b[i],
                                     gmlp_v_g[i], w_spatial[i], b_spatial[i], w_out_ab[i],
                                     ang, c_ckv, c_kpe)
            if ang is None:
                ckv_list.append(ckv)
                kpe_list.append(kpe)
        else:
            mix = pool_mixer(h, w_pool[l // 2], pool_scale[l // 2])
        x = x + g1 * mix
        h = rms_norm(x, norm2_g[l]) * (1 + sc2) + sh2
        x = x + g2 * squared_relu_mlp(h, w_ff1[l], w_ff2[l])
    return rms_norm(x, final_g), ckv_list, kpe_list


def setup_inputs(seed: int = 0) -> dict:
    key = jax.random.key(seed)
    ks = jax.random.split(key, 26)
    f32 = jnp.float32

    def nrm(k, shape, scale):
        return jax.random.normal(k, shape, f32) * scale

    def gain(k, shape):
        return 1.0 + 0.02 * jax.random.normal(k, shape, f32)

    D = D_MODEL
    return {
        'x_prompt': nrm(ks[0], (BATCH, SEQ, D), 1.0),
        'x_sample': nrm(ks[1], (DEC_BATCH, DEC_SEQ, D), 1.0),
        'cache_ckv': nrm(ks[2], (DEC_BATCH, N_AB, PAST_LEN, KV_LORA_RANK), 1.0),
        'cache_kpe': nrm(ks[3], (DEC_BATCH, N_AB, PAST_LEN, QK_ROPE_DIM), 1.0),
        'c': nrm(ks[4], (DEC_BATCH, D), 1.0),
        'c_ctx': nrm(ks[5], (D,), 1.0),
        'w_mod': nrm(ks[6], (DEPTH, D, N_MOD * D), 0.5 * D ** -0.5),
        'b_mod': nrm(ks[7], (DEPTH, N_MOD * D), 0.02),
        'norm1_g': gain(ks[8], (DEPTH, D)),
        'norm2_g': gain(ks[9], (DEPTH, D)),
        'w_in_ab': nrm(ks[10], (N_AB, D, IN_AB), D ** -0.5),
        'q_a_g': gain(ks[11], (N_AB, Q_LORA_RANK)),
        'kv_a_g': gain(ks[12], (N_AB, KV_LORA_RANK)),
        'w_q_b': nrm(ks[13], (N_AB, Q_LORA_RANK, MLA_HEADS * (QK_NOPE_DIM + QK_ROPE_DIM)), Q_LORA_RANK ** -0.5),
        'w_kv_b': nrm(ks[14], (N_AB, KV_LORA_RANK, MLA_HEADS * (QK_NOPE_DIM + V_HEAD_DIM)), KV_LORA_RANK ** -0.5),
        'gmlp_v_g': gain(ks[15], (N_AB, GMLP_WIDTH)),
        'w_spatial': nrm(ks[16], (N_AB, GMLP_GROUPS, GMLP_CHUNK, GMLP_CHUNK), GMLP_CHUNK ** -0.5),
        'b_spatial': gain(ks[17], (N_AB, GMLP_GROUPS, GMLP_CHUNK)),
        'w_out_ab': nrm(ks[18], (N_AB, MIX_AB, D), MIX_AB ** -0.5),
        'w_pool': nrm(ks[19], (N_C, len(POOL_WINDOWS), POOL_GROUP_DIM, POOL_GROUP_DIM), POOL_GROUP_DIM ** -0.5),
        'pool_scale': gain(ks[20], (N_C, D)),
        'w_ff1': nrm(ks[21], (DEPTH, D, D_FF), D ** -0.5),
        'w_ff2': nrm(ks[22], (DEPTH, D_FF, D), D_FF ** -0.5),
        'final_g': gain(ks[23], (D,)),
    }


def reference(x_prompt, x_sample, cache_ckv, cache_kpe, c, c_ctx, w_mod, b_mod, norm1_g, norm2_g,
              w_in_ab, q_a_g, kv_a_g, w_q_b, w_kv_b, gmlp_v_g, w_spatial, b_spatial, w_out_ab,
              w_pool, pool_scale, w_ff1, w_ff2, final_g):
    y_prompt, ckv_list, kpe_list = trunk(
        x_prompt, c_ctx, None, None, None, w_mod, b_mod, norm1_g, norm2_g, w_in_ab, q_a_g, kv_a_g,
        w_q_b, w_kv_b, gmlp_v_g, w_spatial, b_spatial, w_out_ab, w_pool, pool_scale,
        w_ff1, w_ff2, final_g)
    new_ckv = jnp.stack(ckv_list, axis=1)
    new_kpe = jnp.stack(kpe_list, axis=1)
    ang = axial_angles(x_sample.shape[1])
    y_sample, _, _ = trunk(
        x_sample, c, ang, cache_ckv, cache_kpe, w_mod, b_mod, norm1_g, norm2_g, w_in_ab, q_a_g, kv_a_g,
        w_q_b, w_kv_b, gmlp_v_g, w_spatial, b_spatial, w_out_ab, w_pool, pool_scale,
        w_ff1, w_ff2, final_g)
    return (y_prompt, y_sample, new_ckv, new_kpe)
```

```python
import functools

import numpy as np
import jax
import jax.numpy as jnp
from jax import lax
from jax.experimental import pallas as pl
from jax.experimental.pallas import tpu as pltpu

F32 = jnp.float32
BF16 = jnp.bfloat16

D = 1024
DEPTH = 4
N_MOD = 6
HEADS = 8
NOPE = 64
ROPE = 32
VDIM = 64
Q_RANK = 384
KV_RANK = 256
GRID_W = 64
ROPE_BASE = 10000.0
ATTN_SCALE = (NOPE + ROPE) ** -0.5
GM_W = 512
GM_GROUPS = 8
GM_CHUNK = 128
POOL_WINDOWS = (2, 4, 8, 16)
POOL_GD = D // len(POOL_WINDOWS)
D_FF = 4 * D
EPS = 1e-6

N_CTX_TOK = 8192
CTX_SEQ_LEN = 256
DEC_SEQ_LEN = 4096

HEAD_PAD = 128
QK_W = HEADS * HEAD_PAD
MOD_ROWS = 8
HALO = 8

C_QA = 0
C_KVA = C_QA + Q_RANK
C_U = C_KVA + KV_RANK
C_V = C_U + GM_W
C_KP = C_V + GM_W
C_KPS = C_KP + HEAD_PAD
IN_W = C_KPS + HEAD_PAD

TM = 512
TF = 1024
TQ = 256
VMEM_LIMIT = 56 * 1024 * 1024


def _rms(x, g):
    return x * lax.rsqrt(jnp.mean(x * x, axis=-1, keepdims=True) + EPS) * g


def _const_spec(shape):
    nd = len(shape)
    return pl.BlockSpec(shape, lambda *_: (0,) * nd)


def _mod_kernel(cond_ref, w_ref, b_ref, o_ref):
    c = cond_ref[...]
    s = (c / (1.0 + jnp.exp(-c))).astype(BF16)
    w = w_ref[...].astype(BF16)
    o_ref[...] = jnp.dot(s, w, preferred_element_type=F32) + b_ref[...]


def _modulation(cond8, w_mod, b_mod):
    tn = 1536
    nw = N_MOD * D
    return pl.pallas_call(
        _mod_kernel,
        grid=(DEPTH, nw // tn),
        in_specs=[
            pl.BlockSpec((MOD_ROWS, D), lambda l, j: (0, 0)),
            pl.BlockSpec((None, D, tn), lambda l, j: (l, 0, j)),
            pl.BlockSpec((None, 1, tn), lambda l, j: (l, 0, j)),
        ],
        out_specs=pl.BlockSpec((None, MOD_ROWS, tn), lambda l, j: (l, 0, j)),
        out_shape=jax.ShapeDtypeStruct((DEPTH, MOD_ROWS, nw), F32),
        compiler_params=pltpu.CompilerParams(
            dimension_semantics=("arbitrary", "arbitrary"), vmem_limit_bytes=VMEM_LIMIT),
        name="modulation",
    )(cond8, w_mod, b_mod.reshape(DEPTH, 1, nw))


def _in_kernel(x_ref, mod_ref, n1g_ref, wcat_ref, qag_ref, kvag_ref, wqa_ref, wqb_ref,
               wkb_ref, wvb_ref, vg_ref, ws_ref, bs_ref, ca_ref, sb_ref,
               q_ref, k_ref, v_ref, ckv_ref, kpe_ref, gm_ref):
    x = x_ref[...]
    sh1 = mod_ref[:, 0 * D:1 * D]
    sc1 = mod_ref[:, 1 * D:2 * D]
    h = (_rms(x, n1g_ref[...]) * (1.0 + sc1) + sh1).astype(BF16)
    z = jnp.dot(h, wcat_ref[...], preferred_element_type=F32)

    ca = ca_ref[...]
    sb = sb_ref[...]
    ca8 = jnp.concatenate([ca] * HEADS, axis=-1)
    sb8 = jnp.concatenate([sb] * HEADS, axis=-1)

    qn = _rms(z[:, C_QA:C_QA + Q_RANK], qag_ref[...]).astype(BF16)
    qa = jnp.dot(qn, wqa_ref[...], preferred_element_type=F32)
    qb = jnp.dot(qn, wqb_ref[...], preferred_element_type=F32)
    q_ref[...] = ((qa * ca8 + qb * sb8) * ATTN_SCALE).astype(BF16)

    ckv = _rms(z[:, C_KVA:C_KVA + KV_RANK], kvag_ref[...])
    ckv_ref[...] = ckv
    kp = z[:, C_KP:C_KP + HEAD_PAD]
    kpe_ref[...] = kp
    kp_rot = kp * ca + z[:, C_KPS:C_KPS + HEAD_PAD] * sb
    ckv_b = ckv.astype(BF16)
    kn = jnp.dot(ckv_b, wkb_ref[...], preferred_element_type=F32)
    k_ref[...] = (kn + jnp.concatenate([kp_rot] * HEADS, axis=-1)).astype(BF16)
    v_ref[...] = jnp.dot(ckv_b, wvb_ref[...], preferred_element_type=F32).astype(BF16)

    u = jax.nn.gelu(z[:, C_U:C_U + GM_W])
    vv = jax.nn.gelu(z[:, C_V:C_V + GM_W])
    vn = _rms(vv, vg_ref[...])
    lane = lax.broadcasted_iota(jnp.int32, (GM_CHUNK, HEAD_PAD), 1)
    lo = lane < 64
    for c in range(TM // GM_CHUNK):
        rows = slice(c * GM_CHUNK, (c + 1) * GM_CHUNK)
        for j in range(GM_W // 128):
            cols = slice(j * 128, (j + 1) * 128)
            blk = vn[rows, cols]
            rhs = jnp.concatenate([jnp.where(lo, blk, 0.0), jnp.where(lo, 0.0, blk)], axis=0).astype(BF16)
            s = jnp.dot(ws_ref[j], rhs, preferred_element_type=F32) + bs_ref[:, cols]
            gm_ref[rows, cols] = (u[rows, cols] * s).astype(BF16)


def _mod_row(i):
    start = i * TM
    return jnp.where(start < N_CTX_TOK, 0, 1 + (start - N_CTX_TOK) // DEC_SEQ_LEN)


def _rope_blk(i):
    start = i * TM
    return jnp.where(start < N_CTX_TOK, DEC_SEQ_LEN // TM, ((start - N_CTX_TOK) % DEC_SEQ_LEN) // TM)


def _in_proj(x, mod_l, n1g, wcat, qag, kvag, wqa, wqb, wkb, wvb, vg, ws, bs, ca_tab, sb_tab):
    n = x.shape[0]
    row = lambda i: (i, 0)
    out_shapes = (
        jax.ShapeDtypeStruct((n, QK_W), BF16),
        jax.ShapeDtypeStruct((n, QK_W), BF16),
        jax.ShapeDtypeStruct((n, QK_W), BF16),
        jax.ShapeDtypeStruct((n, KV_RANK), F32),
        jax.ShapeDtypeStruct((n, HEAD_PAD), F32),
        jax.ShapeDtypeStruct((n, GM_W), BF16),
    )
    return pl.pallas_call(
        _in_kernel,
        grid=(n // TM,),
        in_specs=[
            pl.BlockSpec((TM, D), row),
            pl.BlockSpec((None, 1, N_MOD * D), lambda i: (_mod_row(i), 0, 0)),
            _const_spec((1, D)),
            _const_spec((D, IN_W)),
            _const_spec((1, Q_RANK)),
            _const_spec((1, KV_RANK)),
            _const_spec((Q_RANK, QK_W)),
            _const_spec((Q_RANK, QK_W)),
            _const_spec((KV_RANK, QK_W)),
            _const_spec((KV_RANK, QK_W)),
            _const_spec((1, GM_W)),
            _const_spec((GM_W // 128, GM_CHUNK, 2 * GM_CHUNK)),
            _const_spec((GM_CHUNK, GM_W)),
            pl.BlockSpec((TM, HEAD_PAD), lambda i: (_rope_blk(i), 0)),
            pl.BlockSpec((TM, HEAD_PAD), lambda i: (_rope_blk(i), 0)),
        ],
        out_specs=[
            pl.BlockSpec((TM, QK_W), row), pl.BlockSpec((TM, QK_W), row), pl.BlockSpec((TM, QK_W), row),
            pl.BlockSpec((TM, KV_RANK), row), pl.BlockSpec((TM, HEAD_PAD), row), pl.BlockSpec((TM, GM_W), row),
        ],
        out_shape=out_shapes,
        compiler_params=pltpu.CompilerParams(dimension_semantics=("arbitrary",), vmem_limit_bytes=VMEM_LIMIT),
        name="in_proj",
    )(x, mod_l, n1g, wcat, qag, kvag, wqa, wqb, wkb, wvb, vg, ws, bs, ca_tab, sb_tab)


def _cache_kernel(ckv_ref, kp_ref, wkb_ref, wvb_ref, k_ref, v_ref):
    ckv_b = ckv_ref[...].astype(BF16)
    kn = jnp.dot(ckv_b, wkb_ref[...], preferred_element_type=F32)
    k_ref[...] = (kn + jnp.concatenate([kp_ref[...]] * HEADS, axis=-1)).astype(BF16)
    v_ref[...] = jnp.dot(ckv_b, wvb_ref[...], preferred_element_type=F32).astype(BF16)


def _cache_keys(ckv, kp_blk, wkb, wvb):
    n = ckv.shape[0]
    tr = 512
    row = lambda i: (i, 0)
    return pl.pallas_call(
        _cache_kernel,
        grid=(n // tr,),
        in_specs=[pl.BlockSpec((tr, KV_RANK), row), pl.BlockSpec((tr, HEAD_PAD), row),
                  _const_spec((KV_RANK, QK_W)), _const_spec((KV_RANK, QK_W))],
        out_specs=[pl.BlockSpec((tr, QK_W), row), pl.BlockSpec((tr, QK_W), row)],
        out_shape=(jax.ShapeDtypeStruct((n, QK_W), BF16), jax.ShapeDtypeStruct((n, QK_W), BF16)),
        compiler_params=pltpu.CompilerParams(dimension_semantics=("arbitrary",), vmem_limit_bytes=VMEM_LIMIT),
        name="cache_keys",
    )(ckv, kp_blk, wkb, wvb)


def _attn_kernel(q_ref, k_ref, v_ref, *rest):
    o_ref = rest[-1]
    acc = None
    for hh in range(2):
        cols = slice(hh * HEAD_PAD, (hh + 1) * HEAD_PAD)
        s = lax.dot_general(q_ref[:, cols], k_ref[:, cols], (((1,), (1,)), ((), ())),
                            preferred_element_type=F32)
        m = jnp.max(s, axis=-1, keepdims=True)
        p = jnp.exp(s - m)
        l = jnp.sum(p, axis=-1, keepdims=True)
        o = jnp.dot(p.astype(BF16), v_ref[:, cols], preferred_element_type=F32) / l
        acc = o if acc is None else acc + o
    o_ref[...] = acc.astype(BF16)


def _attention(q, k, v, prev_out, *, batch, t_len, row0, n_out):
    l_len = k.shape[1]
    qblk0 = row0 // TQ
    nq = t_len // TQ
    in_specs = [
        pl.BlockSpec((TQ, 2 * HEAD_PAD), lambda b, j, qi: (qblk0 + b * nq + qi, j)),
        pl.BlockSpec((None, l_len, 2 * HEAD_PAD), lambda b, j, qi: (b, 0, j)),
        pl.BlockSpec((None, l_len, 2 * HEAD_PAD), lambda b, j, qi: (b, 0, j)),
    ]
    args = [q, k, v]
    aliases = {}
    if prev_out is not None:
        in_specs.append(pl.BlockSpec(memory_space=pl.ANY))
        args.append(prev_out)
        aliases = {3: 0}
    return pl.pallas_call(
        _attn_kernel,
        grid=(batch, HEADS // 2, nq),
        in_specs=in_specs,
        out_specs=pl.BlockSpec((TQ, HEAD_PAD), lambda b, j, qi: (qblk0 + b * nq + qi, j)),
        out_shape=jax.ShapeDtypeStruct((n_out, HEADS * VDIM), BF16),
        input_output_aliases=aliases,
        compiler_params=pltpu.CompilerParams(
            dimension_semantics=("arbitrary", "arbitrary", "arbitrary"), vmem_limit_bytes=VMEM_LIMIT),
        name="attention",
    )(*args)


def _mlp_tail(x1, mod_ref, n2g_ref, w1_ref, w2_ref, fg_ref, o_ref, final):
    sh2 = mod_ref[:, 3 * D:4 * D]
    sc2 = mod_ref[:, 4 * D:5 * D]
    g2 = mod_ref[:, 5 * D:6 * D]
    h = (_rms(x1, n2g_ref[...]) * (1.0 + sc2) + sh2).astype(BF16)
    acc = None
    for c in range(D_FF // TF):
        a = jnp.maximum(jnp.dot(h, w1_ref[:, c * TF:(c + 1) * TF], preferred_element_type=F32), 0.0)
        part = jnp.dot((a * a).astype(BF16), w2_ref[c * TF:(c + 1) * TF, :], preferred_element_type=F32)
        acc = part if acc is None else acc + part
    x2 = x1 + g2 * acc
    if final:
        x2 = _rms(x2, fg_ref[...])
    o_ref[...] = x2


def _ffn_ab_kernel(x_ref, attn_ref, gm_ref, mod_ref, wo_ref, n2g_ref, w1_ref, w2_ref, fg_ref, o_ref, *, final):
    g1 = mod_ref[:, 2 * D:3 * D]
    mix = (jnp.dot(attn_ref[...], wo_ref[0:HEADS * VDIM, :], preferred_element_type=F32)
           + jnp.dot(gm_ref[...], wo_ref[HEADS * VDIM:, :], preferred_element_type=F32))
    x1 = x_ref[...] + g1 * mix
    _mlp_tail(x1, mod_ref, n2g_ref, w1_ref, w2_ref, fg_ref, o_ref, final)


def _ffn_pool_kernel(x_ref, xp_ref, xn_ref, mod_ref, n1g_ref, wp_ref, ps_ref, n2g_ref, w1_ref, w2_ref, fg_ref,
                     o_ref, hf_ref, *, final):
    i = pl.program_id(0)
    sh1 = mod_ref[:, 0 * D:1 * D]
    sc1 = mod_ref[:, 1 * D:2 * D]
    g1 = mod_ref[:, 2 * D:3 * D]
    n1g = n1g_ref[...]
    x = x_ref[...]
    ht = _rms(x, n1g) * (1.0 + sc1) + sh1
    hf_ref[0:HALO, :] = _rms(xp_ref[...], n1g) * (1.0 + sc1) + sh1
    hf_ref[HALO:HALO + TM, :] = ht
    hf_ref[HALO + TM:, :] = _rms(xn_ref[...], n1g) * (1.0 + sc1) + sh1

    seq = jnp.where(i * TM < N_CTX_TOK, CTX_SEQ_LEN, DEC_SEQ_LEN)
    pos = (i * TM + lax.broadcasted_iota(jnp.int32, (TM, POOL_GD), 0)) & (seq - 1)
    parts = []
    for gi, w in enumerate(POOL_WINDOWS):
        half = w // 2
        cols = slice(gi * POOL_GD, (gi + 1) * POOL_GD)
        tot = None
        for d in range(-half, half):
            valid = (pos + d >= 0) & (pos + d < seq)
            term = jnp.where(valid, hf_ref[HALO + d:HALO + d + TM, cols], 0.0)
            tot = term if tot is None else tot + term
        cnt = (jnp.minimum(pos + half, seq) - jnp.maximum(pos - half, 0)).astype(F32)
        diff = (tot / cnt - ht[:, cols]).astype(BF16)
        parts.append(jnp.dot(diff, wp_ref[gi], preferred_element_type=F32))
    mix = jnp.concatenate(parts, axis=-1) * ps_ref[...]
    x1 = x + g1 * mix
    _mlp_tail(x1, mod_ref, n2g_ref, w1_ref, w2_ref, fg_ref, o_ref, final)


def _ffn_common_specs():
    return [_const_spec((1, D)), _const_spec((D, D_FF)), _const_spec((D_FF, D)), _const_spec((1, D))]


def _ffn_ab(x, attn, gm, mod_l, wo, n2g, w1, w2, fg, final):
    n = x.shape[0]
    row = lambda i: (i, 0)
    return pl.pallas_call(
        functools.partial(_ffn_ab_kernel, final=final),
        grid=(n // TM,),
        in_specs=[
            pl.BlockSpec((TM, D), row),
            pl.BlockSpec((TM, HEADS * VDIM), row),
            pl.BlockSpec((TM, GM_W), row),
            pl.BlockSpec((None, 1, N_MOD * D), lambda i: (_mod_row(i), 0, 0)),
            _const_spec((HEADS * VDIM + GM_W, D)),
        ] + _ffn_common_specs(),
        out_specs=pl.BlockSpec((TM, D), row),
        out_shape=jax.ShapeDtypeStruct((n, D), F32),
        compiler_params=pltpu.CompilerParams(dimension_semantics=("arbitrary",), vmem_limit_bytes=VMEM_LIMIT),
        name="ffn_ab",
    )(x, attn, gm, mod_l, wo, n2g, w1, w2, fg)


def _ffn_pool(x, mod_l, n1g, wp, ps, n2g, w1, w2, fg, final):
    n = x.shape[0]
    row = lambda i: (i, 0)
    hb = TM // HALO
    return pl.pallas_call(
        functools.partial(_ffn_pool_kernel, final=final),
        grid=(n // TM,),
        in_specs=[
            pl.BlockSpec((TM, D), row),
            pl.BlockSpec((HALO, D), lambda i: (jnp.maximum(i * hb - 1, 0), 0)),
            pl.BlockSpec((HALO, D), lambda i: (jnp.minimum((i + 1) * hb, n // HALO - 1), 0)),
            pl.BlockSpec((None, 1, N_MOD * D), lambda i: (_mod_row(i), 0, 0)),
            _const_spec((1, D)),
            _const_spec((len(POOL_WINDOWS), POOL_GD, POOL_GD)),
            _const_spec((1, D)),
        ] + _ffn_common_specs(),
        out_specs=pl.BlockSpec((TM, D), row),
        out_shape=jax.ShapeDtypeStruct((n, D), F32),
        scratch_shapes=[pltpu.VMEM((TM + 2 * HALO, D), F32)],
        compiler_params=pltpu.CompilerParams(dimension_semantics=("arbitrary",), vmem_limit_bytes=VMEM_LIMIT),
        name="ffn_pool",
    )(x, x, x, mod_l, n1g, wp, ps, n2g, w1, w2, fg)


def _head_slots(w, src_cols, dst_off, width):
    out = jnp.zeros((w.shape[0], QK_W), w.dtype)
    for h in range(HEADS):
        out = out.at[:, h * HEAD_PAD + dst_off[h]:h * HEAD_PAD + dst_off[h] + width].set(
            w[:, src_cols[h]:src_cols[h] + width])
    return out


_SWAP = np.arange(ROPE).reshape(2, 2, ROPE // 4)[:, ::-1, :].reshape(ROPE)


def _rope_tables(t_len):
    rows = t_len // GRID_W
    row = jnp.repeat(jnp.arange(rows), GRID_W).astype(F32)
    col = jnp.tile(jnp.arange(GRID_W), rows).astype(F32)
    axis_dim = ROPE // 2
    inv = ROPE_BASE ** (-jnp.arange(0, axis_dim, 2, dtype=F32) / axis_dim)
    ang = jnp.stack([row[:, None] * inv, col[:, None] * inv], axis=1)
    cos = jnp.cos(ang)
    sin = jnp.sin(ang)
    c32 = jnp.stack([cos, cos], axis=2).reshape(t_len, ROPE)
    s32 = jnp.stack([-sin, sin], axis=2).reshape(t_len, ROPE)
    ones = jnp.ones((t_len, NOPE), F32)
    zpad = jnp.zeros((t_len, HEAD_PAD - NOPE - ROPE), F32)
    ca = jnp.concatenate([ones, c32, zpad], axis=-1)
    sb = jnp.concatenate([jnp.zeros_like(ones), s32, zpad], axis=-1)
    ca_id = jnp.concatenate([jnp.ones((TM, NOPE + ROPE), F32), jnp.zeros((TM, HEAD_PAD - NOPE - ROPE), F32)], -1)
    return jnp.concatenate([ca, ca_id], 0), jnp.concatenate([sb, jnp.zeros((TM, HEAD_PAD), F32)], 0)


def kernel(x_prompt, x_sample, cache_ckv, cache_kpe, c, c_ctx, w_mod, b_mod, norm1_g, norm2_g, w_in_ab, q_a_g,
           kv_a_g, w_q_b, w_kv_b, gmlp_v_g, w_spatial, b_spatial, w_out_ab, w_pool, pool_scale, w_ff1, w_ff2,
           final_g):
    batch, seq, _ = x_prompt.shape
    dec_batch, dec_seq, _ = x_sample.shape
    past = cache_ckv.shape[2]
    n_ab = w_in_ab.shape[0]
    assert (batch * seq, seq, dec_seq) == (N_CTX_TOK, CTX_SEQ_LEN, DEC_SEQ_LEN)
    n_ctx = batch * seq
    n_lat = dec_batch * dec_seq
    n_tok = n_ctx + n_lat

    x = jnp.concatenate([x_prompt.reshape(n_ctx, D), x_sample.reshape(n_lat, D)], axis=0)
    cond8 = jnp.concatenate([c_ctx[None], c, jnp.zeros((MOD_ROWS - 1 - dec_batch, D), F32)], axis=0)
    mod = _modulation(cond8, w_mod, b_mod).reshape(DEPTH, MOD_ROWS, 1, N_MOD * D)
    ca_tab, sb_tab = _rope_tables(dec_seq)

    kpe_off = KV_RANK + Q_RANK
    new_ckv, new_kpe = [], []
    for l in range(DEPTH):
        final = l == DEPTH - 1
        n2g = norm2_g[l][None]
        w1 = w_ff1[l].astype(BF16)
        w2 = w_ff2[l].astype(BF16)
        fg = final_g[None]
        if l % 2 == 0:
            i = l // 2
            w_in = w_in_ab[i]
            kp_w = w_in[:, kpe_off:kpe_off + ROPE]
            zl = jnp.zeros((D, NOPE), F32)
            zr = jnp.zeros((D, HEAD_PAD - NOPE - ROPE), F32)
            wcat = jnp.concatenate(
                [w_in[:, :Q_RANK], w_in[:, Q_RANK:kpe_off], w_in[:, kpe_off + ROPE:],
                 zl, kp_w, zr, zl, kp_w[:, _SWAP], zr], axis=1).astype(BF16)
            wq = w_q_b[i]
            q_src = [h * (NOPE + ROPE) for h in range(HEADS)]
            wqa = _head_slots(wq, q_src, [0] * HEADS, NOPE + ROPE).astype(BF16)
            wq_sw = wq.reshape(Q_RANK, HEADS, NOPE + ROPE)[:, :, NOPE:][:, :, _SWAP].reshape(Q_RANK, HEADS * ROPE)
            wqb = _head_slots(wq_sw, [h * ROPE for h in range(HEADS)], [NOPE] * HEADS, ROPE).astype(BF16)
            wkv = w_kv_b[i]
            kv_src = [h * (NOPE + VDIM) for h in range(HEADS)]
            wkb = _head_slots(wkv, kv_src, [0] * HEADS, NOPE).astype(BF16)
            wvb = _head_slots(wkv, [s + NOPE for s in kv_src], [(h % 2) * VDIM for h in range(HEADS)],
                              VDIM).astype(BF16)
            ws = w_spatial[i].reshape(GM_GROUPS // 2, 2, GM_CHUNK, GM_CHUNK).transpose(0, 2, 1, 3).reshape(
                GM_GROUPS // 2, GM_CHUNK, 2 * GM_CHUNK).astype(BF16)
            bs = jnp.repeat(b_spatial[i].T, GM_W // GM_GROUPS, axis=1)

            q, k, v, ckv, kpe, gm = _in_proj(
                x, mod[l], norm1_g[l][None], wcat, q_a_g[i][None], kv_a_g[i][None], wqa, wqb, wkb, wvb,
                gmlp_v_g[i][None], ws, bs, ca_tab, sb_tab)
            new_ckv.append(ckv[:n_ctx].reshape(batch, seq, KV_RANK))
            new_kpe.append(kpe[:n_ctx, NOPE:NOPE + ROPE].reshape(batch, seq, ROPE))

            kp_cache = jnp.pad(cache_kpe[:, i].reshape(dec_batch * past, ROPE),
                               ((0, 0), (NOPE, HEAD_PAD - NOPE - ROPE)))
            k_c, v_c = _cache_keys(cache_ckv[:, i].reshape(dec_batch * past, KV_RANK), kp_cache, wkb, wvb)
            k_lat = jnp.concatenate([k_c.reshape(dec_batch, past, QK_W),
                                     k[n_ctx:].reshape(dec_batch, dec_seq, QK_W)], axis=1)
            v_lat = jnp.concatenate([v_c.reshape(dec_batch, past, QK_W),
                                     v[n_ctx:].reshape(dec_batch, dec_seq, QK_W)], axis=1)
            attn = _attention(q, k[:n_ctx].reshape(batch, seq, QK_W), v[:n_ctx].reshape(batch, seq, QK_W), None,
                              batch=batch, t_len=seq, row0=0, n_out=n_tok)
            attn = _attention(q, k_lat, v_lat, attn, batch=dec_batch, t_len=dec_seq, row0=n_ctx, n_out=n_tok)
            x = _ffn_ab(x, attn, gm, mod[l], w_out_ab[i].astype(BF16), n2g, w1, w2, fg, final)
        else:
            i = l // 2
            x = _ffn_pool(x, mod[l], norm1_g[l][None], w_pool[i].astype(BF16), pool_scale[i][None],
                          n2g, w1, w2, fg, final)

    y_prompt = x[:n_ctx].reshape(batch, seq, D)
    y_sample = x[n_ctx:].reshape(dec_batch, dec_seq, D)
    return y_prompt, y_sample, jnp.stack(new_ckv, axis=1), jnp.stack(new_kpe, axis=1)
```

```python
import functools
import math

import numpy as np
import jax
import jax.numpy as jnp
from jax import lax
from jax.experimental import pallas as pl
from jax.experimental.pallas import tpu as pltpu

F32 = jnp.float32
BF16 = jnp.bfloat16

D = 1024
DEPTH = 4
N_MOD = 6
HEADS = 8
NOPE = 64
ROPE = 32
VDIM = 64
Q_RANK = 384
KV_RANK = 256
GRID_W = 64
ROPE_BASE = 10000.0
ATTN_SCALE = (NOPE + ROPE) ** -0.5
GM_W = 512
GM_GROUPS = 8
GM_CHUNK = 128
POOL_WINDOWS = (2, 4, 8, 16)
POOL_GD = D // len(POOL_WINDOWS)
D_FF = 4 * D
EPS = 1e-6
CTX_SEQ_LEN = 256

LANES = 128
HEAD_PAD = LANES
QK_W = HEADS * HEAD_PAD
V_ROWS = VDIM + 16
VT_ROWS = HEADS * V_ROWS
MOD_ROWS = 8
HALO = 8
SEG = 256

C_QA = 0
C_KVA = C_QA + Q_RANK
C_U = C_KVA + KV_RANK
C_V = C_U + GM_W
C_KP = C_V + GM_W
C_KPS = C_KP + HEAD_PAD
C_KO = C_KPS + HEAD_PAD
IN_W = C_KO + HEAD_PAD

TM = 512
TF = 1024
TQ = 256
MAX_KEY_BLK = 1536
VMEM_LIMIT = 56 * 1024 * 1024


def _rms(x, g):
    return x * lax.rsqrt(jnp.mean(x * x, axis=-1, keepdims=True) + EPS) * g


def _const_spec(shape):
    nd = len(shape)
    return pl.BlockSpec(shape, lambda *_: (0,) * nd)


def _params(n_axes):
    return pltpu.CompilerParams(dimension_semantics=("arbitrary",) * n_axes, vmem_limit_bytes=VMEM_LIMIT)


def _mod_kernel(cond_ref, w_ref, b_ref, o_ref):
    c = cond_ref[...]
    s = (c / (1.0 + jnp.exp(-c))).astype(BF16)
    w = w_ref[...].astype(BF16)
    o_ref[...] = jnp.dot(s, w, preferred_element_type=F32) + b_ref[...]


def _modulation(cond8, w_mod, b_mod):
    tn = 1536
    nw = N_MOD * D
    return pl.pallas_call(
        _mod_kernel,
        grid=(DEPTH, nw // tn),
        in_specs=[
            pl.BlockSpec((MOD_ROWS, D), lambda l, j: (0, 0)),
            pl.BlockSpec((None, D, tn), lambda l, j: (l, 0, j)),
            pl.BlockSpec((None, 1, tn), lambda l, j: (l, 0, j)),
        ],
        out_specs=pl.BlockSpec((None, MOD_ROWS, tn), lambda l, j: (l, 0, j)),
        out_shape=jax.ShapeDtypeStruct((DEPTH, MOD_ROWS, nw), F32),
        compiler_params=_params(2),
        name="modulation",
    )(cond8, w_mod, b_mod.reshape(DEPTH, 1, nw))


def _mod_spec(tiles_per_batch):
    if tiles_per_batch is None:
        return pl.BlockSpec((None, 1, N_MOD * D), lambda i: (0, 0, 0))
    return pl.BlockSpec((None, 1, N_MOD * D), lambda i: (1 + i // tiles_per_batch, 0, 0))


def _in_kernel(x_ref, mod_ref, n1g_ref, wcat_ref, qag_ref, kvag_ref, wq_ref, wkb_ref, wvt_ref, aug_ref,
               vg_ref, ws_ref, bs_ref, tq_ref, ck_ref, sk_ref, *out_refs, is_ctx):
    if is_ctx:
        q_ref, k_ref, vt_ref, gm_ref, ckv_ref, kpe_ref = out_refs
    else:
        q_ref, k_ref, vt_ref, gm_ref = out_refs
    x = x_ref[...]
    sh1 = mod_ref[:, 0 * D:1 * D]
    sc1 = mod_ref[:, 1 * D:2 * D]
    h = (_rms(x, n1g_ref[...]) * (1.0 + sc1) + sh1).astype(BF16)
    z = jnp.dot(h, wcat_ref[...], preferred_element_type=F32)

    qn = _rms(z[:, C_QA:C_QA + Q_RANK], qag_ref[...]).astype(BF16)
    qa = jnp.dot(qn, wq_ref[...], preferred_element_type=F32)
    tq8 = jnp.concatenate([tq_ref[...]] * HEADS, axis=-1)
    q_ref[...] = (qa * tq8 * (ATTN_SCALE * math.log2(math.e))).astype(BF16)

    ckv = _rms(z[:, C_KVA:C_KVA + KV_RANK], kvag_ref[...])
    kp_rot = z[:, C_KP:C_KP + HEAD_PAD] * ck_ref[...] + z[:, C_KPS:C_KPS + HEAD_PAD] * sk_ref[...]
    ckv_b = ckv.astype(BF16)
    kn = jnp.dot(ckv_b, wkb_ref[...], preferred_element_type=F32)
    k_ref[...] = (kn + jnp.concatenate([kp_rot] * HEADS, axis=-1)).astype(BF16)
    vt = lax.dot_general(wvt_ref[...], ckv_b, (((1,), (1,)), ((), ())), preferred_element_type=F32)
    vt_ref[...] = (vt + aug_ref[...]).astype(BF16)
    if is_ctx:
        ckv_ref[...] = ckv.reshape(ckv_ref.shape)
        kpe_ref[...] = z[:, C_KO:C_KO + ROPE].reshape(kpe_ref.shape)

    u = jax.nn.gelu(z[:, C_U:C_U + GM_W])
    vv = jax.nn.gelu(z[:, C_V:C_V + GM_W])
    vn = _rms(vv, vg_ref[...])
    lane = lax.broadcasted_iota(jnp.int32, (GM_CHUNK, LANES), 1)
    lo = lane < GM_W // GM_GROUPS
    for c in range(TM // GM_CHUNK):
        rows = slice(c * GM_CHUNK, (c + 1) * GM_CHUNK)
        for j in range(GM_W // LANES):
            cols = slice(j * LANES, (j + 1) * LANES)
            blk = vn[rows, cols]
            rhs = jnp.concatenate([jnp.where(lo, blk, 0.0), jnp.where(lo, 0.0, blk)], axis=0).astype(BF16)
            s = jnp.dot(ws_ref[j], rhs, preferred_element_type=F32) + bs_ref[:, cols]
            gm_ref[rows, cols] = (u[rows, cols] * s).astype(BF16)


def _in_proj(x, mod_l, w, tabs, *, layer, ckv_prev=None, kpe_prev=None, lat_shape=None):
    n = x.shape[0]
    is_ctx = lat_shape is None
    row = lambda i: (i, 0)
    id_blk = tabs[0].shape[0] // TM - 1
    if is_ctx:
        tpb = None
        tab_idx = lambda i: (id_blk, 0)
        k_shape, k_spec = (n, QK_W), pl.BlockSpec((TM, QK_W), row)
        vt_shape, vt_spec = (VT_ROWS, n), pl.BlockSpec((VT_ROWS, TM), lambda i: (0, i))
    else:
        dec_batch, dec_seq, past = lat_shape
        tpb = dec_seq // TM
        pb = past // TM
        tab_idx = lambda i: (i % tpb, 0)
        k_shape = (dec_batch, past + dec_seq, QK_W)
        k_spec = pl.BlockSpec((None, TM, QK_W), lambda i: (i // tpb, pb + i % tpb, 0))
        vt_shape = (dec_batch, VT_ROWS, past + dec_seq)
        vt_spec = pl.BlockSpec((None, VT_ROWS, TM), lambda i: (i // tpb, 0, pb + i % tpb))
    in_specs = [
        pl.BlockSpec((TM, D), row),
        _mod_spec(tpb),
        _const_spec((1, D)),
        _const_spec((D, IN_W)),
        _const_spec((1, Q_RANK)),
        _const_spec((1, KV_RANK)),
        _const_spec((Q_RANK, QK_W)),
        _const_spec((KV_RANK, QK_W)),
        _const_spec((VT_ROWS, KV_RANK)),
        _const_spec((VT_ROWS, 1)),
        _const_spec((1, GM_W)),
        _const_spec((GM_W // LANES, GM_CHUNK, 2 * GM_CHUNK)),
        _const_spec((GM_CHUNK, GM_W)),
        pl.BlockSpec((TM, HEAD_PAD), tab_idx),
        pl.BlockSpec((TM, HEAD_PAD), tab_idx),
        pl.BlockSpec((TM, HEAD_PAD), tab_idx),
    ]
    args = [x, mod_l, w["n1g"], w["wcat"], w["qag"], w["kvag"], w["wq"], w["wkb"], w["wvt"], w["aug"],
            w["vg"], w["ws"], w["bs"], *tabs]
    out_shape = [jax.ShapeDtypeStruct((n, QK_W), BF16), jax.ShapeDtypeStruct(k_shape, BF16),
                 jax.ShapeDtypeStruct(vt_shape, BF16), jax.ShapeDtypeStruct((n, GM_W), BF16)]
    out_specs = [pl.BlockSpec((TM, QK_W), row), k_spec, vt_spec, pl.BlockSpec((TM, GM_W), row)]
    aliases = {}
    if is_ctx:
        seqs = TM // CTX_SEQ_LEN
        batch = n // CTX_SEQ_LEN
        n_ab = DEPTH // 2
        out_shape += [jax.ShapeDtypeStruct((batch, n_ab, CTX_SEQ_LEN, KV_RANK), F32),
                      jax.ShapeDtypeStruct((batch, n_ab, CTX_SEQ_LEN, ROPE), F32)]
        out_specs += [pl.BlockSpec((seqs, None, CTX_SEQ_LEN, KV_RANK), lambda i: (i, layer, 0, 0)),
                      pl.BlockSpec((seqs, None, CTX_SEQ_LEN, ROPE), lambda i: (i, layer, 0, 0))]
        if ckv_prev is not None:
            in_specs += [pl.BlockSpec(memory_space=pl.ANY), pl.BlockSpec(memory_space=pl.ANY)]
            args += [ckv_prev, kpe_prev]
            aliases = {len(args) - 2: 4, len(args) - 1: 5}

    def body(*refs):
        n_in = 16
        _in_kernel(*refs[:n_in], *refs[len(args):], is_ctx=is_ctx)

    return pl.pallas_call(
        body,
        grid=(n // TM,),
        in_specs=in_specs,
        out_specs=out_specs,
        out_shape=out_shape,
        input_output_aliases=aliases,
        compiler_params=_params(1),
        name="in_proj_ctx" if is_ctx else "in_proj_lat",
    )(*args)


def _cache_kernel(ckv_ref, kpe_ref, wkb_ref, wvt_ref, aug_ref, place_ref, k_in, vt_in, k_ref, vt_ref):
    del k_in, vt_in
    ckv_b = ckv_ref[...].astype(BF16)
    kn = jnp.dot(ckv_b, wkb_ref[...], preferred_element_type=F32)
    kp = jnp.dot(kpe_ref[...].astype(BF16), place_ref[...], preferred_element_type=F32)
    k_ref[...] = (kn + jnp.concatenate([kp] * HEADS, axis=-1)).astype(BF16)
    vt = lax.dot_general(wvt_ref[...], ckv_b, (((1,), (1,)), ((), ())), preferred_element_type=F32)
    vt_ref[...] = (vt + aug_ref[...]).astype(BF16)


def _cache_keys(cache_ckv, cache_kpe, w, place, k_lat, vt_lat, *, layer):
    dec_batch, _, past, _ = cache_ckv.shape
    return pl.pallas_call(
        _cache_kernel,
        grid=(dec_batch,),
        in_specs=[
            pl.BlockSpec((None, None, past, KV_RANK), lambda b: (b, layer, 0, 0)),
            pl.BlockSpec((None, None, past, ROPE), lambda b: (b, layer, 0, 0)),
            _const_spec((KV_RANK, QK_W)),
            _const_spec((VT_ROWS, KV_RANK)),
            _const_spec((VT_ROWS, 1)),
            _const_spec((ROPE, HEAD_PAD)),
            pl.BlockSpec(memory_space=pl.ANY),
            pl.BlockSpec(memory_space=pl.ANY),
        ],
        out_specs=[pl.BlockSpec((None, past, QK_W), lambda b: (b, 0, 0)),
                   pl.BlockSpec((None, VT_ROWS, past), lambda b: (b, 0, 0))],
        out_shape=(jax.ShapeDtypeStruct(k_lat.shape, BF16), jax.ShapeDtypeStruct(vt_lat.shape, BF16)),
        input_output_aliases={6: 0, 7: 1},
        compiler_params=_params(1),
        name="cache_keys",
    )(cache_ckv, cache_kpe, w["wkb"], w["wvt"], w["aug"], place, k_lat, vt_lat)


def _attn_kernel(q_ref, k_ref, vt_ref, o_ref, st_ref, p_ref, *, key_blk):
    l_len = k_ref.shape[0]
    n_blk = l_len // key_blk
    blocks = [slice(j * key_blk, (j + 1) * key_blk) for j in range(n_blk)]
    m = [None] * HEADS
    pair = {}
    for step in range(HEADS + 2):
        ha, hb, hc = step, step - 1, step - 2
        mrun = None
        for j, rows in enumerate(blocks):
            if ha < HEADS:
                cols = slice(ha * HEAD_PAD, (ha + 1) * HEAD_PAD)
                st = lax.dot_general(k_ref[rows, cols], q_ref[:, cols], (((1,), (1,)), ((), ())),
                                     preferred_element_type=F32)
                st_ref[ha % 2, rows, :] = st
                blk_max = jnp.max(st.reshape(key_blk // 8, 8, TQ), axis=0)
                mrun = blk_max if mrun is None else jnp.maximum(mrun, blk_max)
            if 0 <= hb < HEADS:
                p_ref[hb % 2, rows, :] = jnp.exp2(st_ref[hb % 2, rows, :] - m[hb]).astype(BF16)
        if ha < HEADS:
            m[ha] = jnp.max(mrun, axis=0, keepdims=True)
        if 0 <= hc < HEADS:
            ot = jnp.dot(vt_ref[hc * V_ROWS:(hc + 1) * V_ROWS, :], p_ref[hc % 2],
                         preferred_element_type=F32)
            pair[hc] = ot[0:VDIM] * (1.0 / ot[VDIM:VDIM + 1])
            if hc % 2 == 1:
                both = jnp.concatenate([pair.pop(hc - 1), pair.pop(hc)], axis=0)
                o_ref[:, (hc // 2) * LANES:(hc // 2 + 1) * LANES] = both.T.astype(BF16)


def _attention(q, k, vt, *, batch, t_len):
    l_len = k.shape[1]
    nq = t_len // TQ
    key_blk = math.gcd(l_len, MAX_KEY_BLK)
    if vt.ndim == 3:
        vt_spec = pl.BlockSpec((None, VT_ROWS, l_len), lambda b, qi: (b, 0, 0))
    else:
        vt_spec = pl.BlockSpec((VT_ROWS, l_len), lambda b, qi: (0, b))
    return pl.pallas_call(
        functools.partial(_attn_kernel, key_blk=key_blk),
        grid=(batch, nq),
        scratch_shapes=[pltpu.VMEM((2, l_len, TQ), F32), pltpu.VMEM((2, l_len, TQ), BF16)],
        in_specs=[
            pl.BlockSpec((TQ, QK_W), lambda b, qi: (b * nq + qi, 0)),
            pl.BlockSpec((None, l_len, QK_W), lambda b, qi: (b, 0, 0)),
            vt_spec,
        ],
        out_specs=pl.BlockSpec((TQ, HEADS * VDIM), lambda b, qi: (b * nq + qi, 0)),
        out_shape=jax.ShapeDtypeStruct((batch * t_len, HEADS * VDIM), BF16),
        compiler_params=_params(2),
        name="attention",
    )(q, k, vt)


def _mlp_tail(x1, mod_ref, n2g_ref, w1_ref, w2_ref, fg_ref, o_ref, final):
    sh2 = mod_ref[:, 3 * D:4 * D]
    sc2 = mod_ref[:, 4 * D:5 * D]
    g2 = mod_ref[:, 5 * D:6 * D]
    h = (_rms(x1, n2g_ref[...]) * (1.0 + sc2) + sh2).astype(BF16)
    acc = None
    for c in range(D_FF // TF):
        a = jnp.maximum(jnp.dot(h, w1_ref[:, c * TF:(c + 1) * TF], preferred_element_type=F32), 0.0)
        part = jnp.dot((a * a).astype(BF16), w2_ref[c * TF:(c + 1) * TF, :], preferred_element_type=F32)
        acc = part if acc is None else acc + part
    x2 = x1 + g2 * acc
    if final:
        x2 = _rms(x2, fg_ref[...])
    o_ref[...] = x2


def _ffn_ab_kernel(x_ref, attn_ref, gm_ref, mod_ref, wo_ref, n2g_ref, w1_ref, w2_ref, fg_ref, o_ref, *, final):
    g1 = mod_ref[:, 2 * D:3 * D]
    mix = (jnp.dot(attn_ref[...], wo_ref[0:HEADS * VDIM, :], preferred_element_type=F32)
           + jnp.dot(gm_ref[...], wo_ref[HEADS * VDIM:, :], preferred_element_type=F32))
    x1 = x_ref[...] + g1 * mix
    _mlp_tail(x1, mod_ref, n2g_ref, w1_ref, w2_ref, fg_ref, o_ref, final)


def _ffn_pool_kernel(x_ref, xp_ref, xn_ref, mod_ref, n1g_ref, band_ref, wp_ref, ps_ref, n2g_ref, w1_ref, w2_ref,
                     fg_ref, o_ref, *, final, tiles_per_seq):
    sh1 = mod_ref[:, 0 * D:1 * D]
    sc1 = mod_ref[:, 1 * D:2 * D]
    g1 = mod_ref[:, 2 * D:3 * D]
    n1g = n1g_ref[...]
    ada = lambda v: _rms(v, n1g) * (1.0 + sc1) + sh1
    x = x_ref[...]
    ht = ada(x)
    zero_halo = jnp.zeros((HALO, D), F32)
    if tiles_per_seq is None:
        seq_len, pos0 = SEG, 0
    else:
        t = pl.program_id(0) % tiles_per_seq
        seq_len, pos0 = tiles_per_seq * TM, t * TM
        tile_prev = jnp.where(t != 0, ada(xp_ref[...]), 0.0)
        tile_next = jnp.where(t != tiles_per_seq - 1, ada(xn_ref[...]), 0.0)
    n_seg = TM // SEG
    seg_rows = []
    for s in range(n_seg):
        seg = ht[s * SEG:(s + 1) * SEG]
        if tiles_per_seq is None:
            prev, nxt, base = zero_halo, zero_halo, 0
        else:
            prev = tile_prev if s == 0 else ht[s * SEG - HALO:s * SEG]
            nxt = tile_next if s == n_seg - 1 else ht[(s + 1) * SEG:(s + 1) * SEG + HALO]
            base = pos0 + s * SEG
        hs = jnp.concatenate([prev, seg, nxt], axis=0)
        hi = hs.astype(BF16)
        lo = (hs - hi.astype(F32)).astype(BF16)
        pos = base + lax.broadcasted_iota(jnp.int32, (SEG, POOL_GD), 0)
        parts = []
        for gi, w in enumerate(POOL_WINDOWS):
            half = w // 2
            cols = slice(gi * POOL_GD, (gi + 1) * POOL_GD)
            top = (jnp.dot(band_ref[gi, 0], hi[0:SEG, cols], preferred_element_type=F32)
                   + jnp.dot(band_ref[gi, 0], lo[0:SEG, cols], preferred_element_type=F32))
            bot = (jnp.dot(band_ref[gi, 1], hi[2 * HALO:, cols], preferred_element_type=F32)
                   + jnp.dot(band_ref[gi, 1], lo[2 * HALO:, cols], preferred_element_type=F32))
            tot = jnp.concatenate([top, bot], axis=0)
            cnt = (jnp.minimum(pos + half, seq_len) - jnp.maximum(pos - half, 0)).astype(F32)
            diff = (tot / cnt - seg[:, cols]).astype(BF16)
            parts.append(jnp.dot(diff, wp_ref[gi], preferred_element_type=F32))
        seg_rows.append(jnp.concatenate(parts, axis=-1))
    mix = jnp.concatenate(seg_rows, axis=0) * ps_ref[...]
    x1 = x + g1 * mix
    _mlp_tail(x1, mod_ref, n2g_ref, w1_ref, w2_ref, fg_ref, o_ref, final)


def _pool_bands():
    half_rows = SEG // 2
    t = np.arange(half_rows)[:, None]
    u = np.arange(SEG)[None, :]
    bands = np.zeros((len(POOL_WINDOWS), 2, half_rows, SEG), np.float32)
    for gi, w in enumerate(POOL_WINDOWS):
        half = w // 2
        d_top = (u - HALO) - t
        d_bot = (u + HALO) - (half_rows + t)
        bands[gi, 0] = (d_top >= -half) & (d_top <= half - 1)
        bands[gi, 1] = (d_bot >= -half) & (d_bot <= half - 1)
    return jnp.asarray(bands, BF16)


def _ffn_common_specs():
    return [_const_spec((1, D)), _const_spec((D, D_FF)), _const_spec((D_FF, D)), _const_spec((1, D))]


def _ffn_ab(x, attn, gm, mod_l, w, fg, final, tiles_per_batch):
    n = x.shape[0]
    row = lambda i: (i, 0)
    return pl.pallas_call(
        functools.partial(_ffn_ab_kernel, final=final),
        grid=(n // TM,),
        in_specs=[
            pl.BlockSpec((TM, D), row),
            pl.BlockSpec((TM, HEADS * VDIM), row),
            pl.BlockSpec((TM, GM_W), row),
            _mod_spec(tiles_per_batch),
            _const_spec((HEADS * VDIM + GM_W, D)),
        ] + _ffn_common_specs(),
        out_specs=pl.BlockSpec((TM, D), row),
        out_shape=jax.ShapeDtypeStruct((n, D), F32),
        compiler_params=_params(1),
        name="ffn_ab",
    )(x, attn, gm, mod_l, w["wo"], w["n2g"], w["w1"], w["w2"], fg)


def _ffn_pool(x, mod_l, w, bands, fg, final, tiles_per_batch):
    n = x.shape[0]
    row = lambda i: (i, 0)
    hb = TM // HALO
    return pl.pallas_call(
        functools.partial(_ffn_pool_kernel, final=final, tiles_per_seq=tiles_per_batch),
        grid=(n // TM,),
        in_specs=[
            pl.BlockSpec((TM, D), row),
            pl.BlockSpec((HALO, D), lambda i: (jnp.maximum(i * hb - 1, 0), 0)),
            pl.BlockSpec((HALO, D), lambda i: (jnp.minimum((i + 1) * hb, n // HALO - 1), 0)),
            _mod_spec(tiles_per_batch),
            _const_spec((1, D)),
            _const_spec((len(POOL_WINDOWS), 2, SEG // 2, SEG)),
            _const_spec((len(POOL_WINDOWS), POOL_GD, POOL_GD)),
            _const_spec((1, D)),
        ] + _ffn_common_specs(),
        out_specs=pl.BlockSpec((TM, D), row),
        out_shape=jax.ShapeDtypeStruct((n, D), F32),
        compiler_params=_params(1),
        name="ffn_pool",
    )(x, x, x, mod_l, w["n1g"], bands, w["wp"], w["ps"], w["n2g"], w["w1"], w["w2"], fg)


_SWAP = np.arange(ROPE).reshape(2, 2, ROPE // 4)[:, ::-1, :].reshape(ROPE)


def _rope_tables(t_len):
    rows = t_len // GRID_W
    row = jnp.repeat(jnp.arange(rows), GRID_W).astype(F32)
    col = jnp.tile(jnp.arange(GRID_W), rows).astype(F32)
    axis_dim = ROPE // 2
    inv = ROPE_BASE ** (-jnp.arange(0, axis_dim, 2, dtype=F32) / axis_dim)
    ang = jnp.stack([row[:, None] * inv, col[:, None] * inv], axis=1)
    cos = jnp.cos(ang)
    sin = jnp.sin(ang)
    c32 = jnp.stack([cos, cos], axis=2).reshape(t_len, ROPE)
    s32 = jnp.stack([-sin, sin], axis=2).reshape(t_len, ROPE)
    one = lambda r, c: jnp.ones((r, c), F32)
    zero = lambda r, c: jnp.zeros((r, c), F32)
    tq = jnp.concatenate([jnp.concatenate([one(t_len, NOPE), c32, s32], -1),
                          jnp.concatenate([one(TM, NOPE + ROPE), zero(TM, ROPE)], -1)], 0)
    ck = jnp.concatenate([jnp.concatenate([zero(t_len, NOPE), c32, c32], -1),
                          jnp.concatenate([zero(TM, NOPE), one(TM, 2 * ROPE)], -1)], 0)
    sk = jnp.concatenate([jnp.concatenate([zero(t_len, NOPE), s32, s32], -1), zero(TM, HEAD_PAD)], 0)
    return tq, ck, sk


def _prep_even_weights(w_in_ab, q_a_g, kv_a_g, w_q_b, w_kv_b, gmlp_v_g, w_spatial, b_spatial, w_out_ab):
    n_ab = w_in_ab.shape[0]
    kpe_off = Q_RANK + KV_RANK
    kp_w = w_in_ab[:, :, kpe_off:kpe_off + ROPE]
    kp_sw = kp_w[:, :, _SWAP]
    z = lambda c: jnp.zeros((n_ab, D, c), F32)
    wcat = jnp.concatenate(
        [w_in_ab[:, :, :kpe_off], w_in_ab[:, :, kpe_off + ROPE:],
         z(NOPE), kp_w, kp_w, z(NOPE), kp_sw, kp_sw, kp_w, z(HEAD_PAD - ROPE)], axis=2).astype(BF16)
    wq4 = w_q_b.reshape(n_ab, Q_RANK, HEADS, NOPE + ROPE)
    wq = jnp.concatenate([wq4, wq4[..., NOPE:][..., _SWAP]], axis=-1).reshape(n_ab, Q_RANK, QK_W).astype(BF16)
    wkv4 = w_kv_b.reshape(n_ab, KV_RANK, HEADS, NOPE + VDIM)
    wkb = jnp.pad(wkv4[..., :NOPE], ((0, 0), (0, 0), (0, 0), (0, HEAD_PAD - NOPE))).reshape(
        n_ab, KV_RANK, QK_W).astype(BF16)
    wvt = jnp.pad(wkv4[..., NOPE:].transpose(0, 2, 3, 1), ((0, 0), (0, 0), (0, V_ROWS - VDIM), (0, 0))).reshape(
        n_ab, VT_ROWS, KV_RANK).astype(BF16)
    ws = w_spatial.reshape(n_ab, GM_GROUPS // 2, 2, GM_CHUNK, GM_CHUNK).transpose(0, 1, 3, 2, 4).reshape(
        n_ab, GM_GROUPS // 2, GM_CHUNK, 2 * GM_CHUNK).astype(BF16)
    bs = jnp.repeat(b_spatial.transpose(0, 2, 1), GM_W // GM_GROUPS, axis=2)
    aug = np.zeros((VT_ROWS, 1), np.float32)
    aug[VDIM::V_ROWS] = 1.0
    return dict(wcat=wcat, wq=wq, wkb=wkb, wvt=wvt, ws=ws, bs=bs, aug=jnp.asarray(aug),
                qag=q_a_g[:, None], kvag=kv_a_g[:, None], vg=gmlp_v_g[:, None], wo=w_out_ab.astype(BF16))


def kernel(x_prompt, x_sample, cache_ckv, cache_kpe, c, c_ctx, w_mod, b_mod, norm1_g, norm2_g, w_in_ab, q_a_g,
           kv_a_g, w_q_b, w_kv_b, gmlp_v_g, w_spatial, b_spatial, w_out_ab, w_pool, pool_scale, w_ff1, w_ff2,
           final_g):
    batch, seq, _ = x_prompt.shape
    dec_batch, dec_seq, _ = x_sample.shape
    past = cache_ckv.shape[2]
    assert seq == CTX_SEQ_LEN and dec_seq % TM == 0 and past % TM == 0
    lat_tiles = dec_seq // TM

    xc = x_prompt.reshape(batch * seq, D)
    xl = x_sample.reshape(dec_batch * dec_seq, D)
    cond8 = jnp.concatenate([c_ctx[None], c, jnp.zeros((MOD_ROWS - 1 - dec_batch, D), F32)], axis=0)
    mod = _modulation(cond8, w_mod, b_mod).reshape(DEPTH, MOD_ROWS, 1, N_MOD * D)
    tabs = _rope_tables(dec_seq)
    bands = _pool_bands()
    place = np.zeros((ROPE, HEAD_PAD), np.float32)
    place[np.arange(ROPE), NOPE + np.arange(ROPE)] = 1.0
    place[np.arange(ROPE), NOPE + ROPE + np.arange(ROPE)] = 1.0
    place = jnp.asarray(place, BF16)

    ev = _prep_even_weights(w_in_ab, q_a_g, kv_a_g, w_q_b, w_kv_b, gmlp_v_g, w_spatial, b_spatial, w_out_ab)
    w1 = w_ff1.astype(BF16)
    w2 = w_ff2.astype(BF16)
    wp = w_pool.astype(BF16)
    fg = final_g[None]

    new_ckv = new_kpe = None
    for l in range(DEPTH):
        final = l == DEPTH - 1
        i = l // 2
        w = dict(n1g=norm1_g[l][None], n2g=norm2_g[l][None], w1=w1[l], w2=w2[l])
        if l % 2 == 0:
            w.update({name: val[i] for name, val in ev.items() if name != "aug"}, aug=ev["aug"])
            qc, kc, vtc, gmc, new_ckv, new_kpe = _in_proj(
                xc, mod[l], w, tabs, layer=i, ckv_prev=new_ckv, kpe_prev=new_kpe)
            ql, kl, vtl, gml = _in_proj(xl, mod[l], w, tabs, layer=i, lat_shape=(dec_batch, dec_seq, past))
            kl, vtl = _cache_keys(cache_ckv, cache_kpe, w, place, kl, vtl, layer=i)
            attn_c = _attention(qc, kc.reshape(batch, seq, QK_W), vtc, batch=batch, t_len=seq)
            attn_l = _attention(ql, kl, vtl, batch=dec_batch, t_len=dec_seq)
            xc = _ffn_ab(xc, attn_c, gmc, mod[l], w, fg, final, None)
            xl = _ffn_ab(xl, attn_l, gml, mod[l], w, fg, final, lat_tiles)
        else:
            w.update(wp=wp[i], ps=pool_scale[i][None])
            xc = _ffn_pool(xc, mod[l], w, bands, fg, final, None)
            xl = _ffn_pool(xl, mod[l], w, bands, fg, final, lat_tiles)

    return xc.reshape(batch, seq, D), xl.reshape(dec_batch, dec_seq, D), new_ckv, new_kpe
```

```python
import functools
import math

import numpy as np
import jax
import jax.numpy as jnp
from jax import lax
from jax.experimental import pallas as pl
from jax.experimental.pallas import tpu as pltpu

F32 = jnp.float32
BF16 = jnp.bfloat16

D = 1024
DEPTH = 4
N_MOD = 6
HEADS = 8
NOPE = 64
ROPE = 32
VDIM = 64
Q_RANK = 384
KV_RANK = 256
GRID_W = 64
ROPE_BASE = 10000.0
ATTN_SCALE = (NOPE + ROPE) ** -0.5
GM_W = 512
GM_GROUPS = 8
GM_CHUNK = 128
POOL_WINDOWS = (2, 4, 8, 16)
POOL_GD = D // len(POOL_WINDOWS)
D_FF = 4 * D
EPS = 1e-6
CTX_SEQ_LEN = 256

LANES = 128
HEAD_PAD = LANES
QK_W = HEADS * HEAD_PAD
V_ROWS = VDIM + 16
VT_ROWS = HEADS * V_ROWS
MOD_ROWS = 8
HALO = 8
SEG = 256

C_QA = 0
C_KVA = C_QA + Q_RANK
C_U = C_KVA + KV_RANK
C_V = C_U + GM_W
C_KP = C_V + GM_W
C_KPS = C_KP + HEAD_PAD
C_KO = C_KPS + HEAD_PAD
IN_W = C_KO + HEAD_PAD

TM = 512
TF = 1024
TQ = 256
Q_TILES = 2
MAX_KEY_BLK = 1536
VMEM_LIMIT = 56 * 1024 * 1024


def _rms(x, g):
    return x * lax.rsqrt(jnp.mean(x * x, axis=-1, keepdims=True) + EPS) * g


def _const_spec(shape):
    nd = len(shape)
    return pl.BlockSpec(shape, lambda *_: (0,) * nd)


def _weights(w, names):
    arrs, specs = [], []
    for name in names:
        entry = w[name]
        if isinstance(entry, tuple):
            arr, layer = entry
            shape = arr.shape[1:]
            specs.append(pl.BlockSpec((None,) + shape, lambda *_, layer=layer, nd=len(shape): (layer,) + (0,) * nd))
        else:
            arr = entry
            specs.append(_const_spec(arr.shape))
        arrs.append(arr)
    return arrs, specs


def _params(n_axes, flags=None):
    return pltpu.CompilerParams(dimension_semantics=("arbitrary",) * n_axes, vmem_limit_bytes=VMEM_LIMIT,
                                flags=flags)


def _mod_kernel(cond_ref, w_ref, b_ref, o_ref):
    c = cond_ref[...]
    s = (c / (1.0 + jnp.exp(-c))).astype(BF16)
    w = w_ref[...].astype(BF16)
    o_ref[...] = jnp.dot(s, w, preferred_element_type=F32) + b_ref[...]


def _modulation(cond8, w_mod, b_mod):
    tn = 1536
    nw = N_MOD * D
    return pl.pallas_call(
        _mod_kernel,
        grid=(DEPTH, nw // tn),
        in_specs=[
            pl.BlockSpec((MOD_ROWS, D), lambda l, j: (0, 0)),
            pl.BlockSpec((None, D, tn), lambda l, j: (l, 0, j)),
            pl.BlockSpec((None, 1, tn), lambda l, j: (l, 0, j)),
        ],
        out_specs=pl.BlockSpec((None, MOD_ROWS, tn), lambda l, j: (l, 0, j)),
        out_shape=jax.ShapeDtypeStruct((DEPTH, MOD_ROWS, nw), F32),
        compiler_params=_params(2),
        name="modulation",
    )(cond8, w_mod, b_mod.reshape(DEPTH, 1, nw))


def _mod_spec(tiles_per_batch):
    if tiles_per_batch is None:
        return pl.BlockSpec((None, 1, N_MOD * D), lambda i: (0, 0, 0))
    return pl.BlockSpec((None, 1, N_MOD * D), lambda i: (1 + i // tiles_per_batch, 0, 0))


def _in_kernel(x_ref, mod_ref, n1g_ref, wcat_ref, qag_ref, kvag_ref, wq_ref, wkb_ref, wvt_ref, aug_ref,
               vg_ref, ws_ref, bs_ref, tq_ref, ck_ref, sk_ref, *out_refs, is_ctx):
    if is_ctx:
        q_ref, k_ref, vt_ref, gm_ref, ckv_ref, kpe_ref = out_refs
    else:
        q_ref, k_ref, vt_ref, gm_ref = out_refs
    x = x_ref[...]
    sh1 = mod_ref[:, 0 * D:1 * D]
    sc1 = mod_ref[:, 1 * D:2 * D]
    h = (_rms(x, n1g_ref[...]) * (1.0 + sc1) + sh1).astype(BF16)
    z = jnp.dot(h, wcat_ref[...], preferred_element_type=F32)

    qn = _rms(z[:, C_QA:C_QA + Q_RANK], qag_ref[...]).astype(BF16)
    qa = jnp.dot(qn, wq_ref[...], preferred_element_type=F32)
    tq8 = jnp.concatenate([tq_ref[...]] * HEADS, axis=-1)
    q_ref[...] = (qa * tq8 * (ATTN_SCALE * math.log2(math.e))).astype(BF16)

    ckv = _rms(z[:, C_KVA:C_KVA + KV_RANK], kvag_ref[...])
    kp_rot = z[:, C_KP:C_KP + HEAD_PAD] * ck_ref[...] + z[:, C_KPS:C_KPS + HEAD_PAD] * sk_ref[...]
    ckv_b = ckv.astype(BF16)
    kn = jnp.dot(ckv_b, wkb_ref[...], preferred_element_type=F32)
    k_ref[...] = (kn + jnp.concatenate([kp_rot] * HEADS, axis=-1)).astype(BF16)
    vt = lax.dot_general(wvt_ref[...], ckv_b, (((1,), (1,)), ((), ())), preferred_element_type=F32)
    vt_ref[...] = (vt + aug_ref[...]).astype(BF16)
    if is_ctx:
        ckv_ref[...] = ckv.reshape(ckv_ref.shape)
        kpe_ref[...] = z[:, C_KO:C_KO + ROPE].reshape(kpe_ref.shape)

    u = jax.nn.gelu(z[:, C_U:C_U + GM_W])
    vv = jax.nn.gelu(z[:, C_V:C_V + GM_W])
    vn = _rms(vv, vg_ref[...])
    lane = lax.broadcasted_iota(jnp.int32, (GM_CHUNK, LANES), 1)
    lo = lane < GM_W // GM_GROUPS
    for c in range(TM // GM_CHUNK):
        rows = slice(c * GM_CHUNK, (c + 1) * GM_CHUNK)
        for j in range(GM_W // LANES):
            cols = slice(j * LANES, (j + 1) * LANES)
            blk = vn[rows, cols]
            rhs = jnp.concatenate([jnp.where(lo, blk, 0.0), jnp.where(lo, 0.0, blk)], axis=0).astype(BF16)
            s = jnp.dot(ws_ref[j], rhs, preferred_element_type=F32) + bs_ref[:, cols]
            gm_ref[rows, cols] = (u[rows, cols] * s).astype(BF16)


def _in_proj(x, mod_l, w, tabs, *, layer, ckv_prev=None, kpe_prev=None, lat_shape=None):
    n = x.shape[0]
    is_ctx = lat_shape is None
    row = lambda i: (i, 0)
    id_blk = tabs[0].shape[0] // TM - 1
    if is_ctx:
        tpb = None
        tab_idx = lambda i: (id_blk, 0)
        k_shape, k_spec = (n, QK_W), pl.BlockSpec((TM, QK_W), row)
        vt_shape, vt_spec = (VT_ROWS, n), pl.BlockSpec((VT_ROWS, TM), lambda i: (0, i))
    else:
        dec_batch, dec_seq, past = lat_shape
        tpb = dec_seq // TM
        pb = past // TM
        tab_idx = lambda i: (i % tpb, 0)
        k_shape = (dec_batch, past + dec_seq, QK_W)
        k_spec = pl.BlockSpec((None, TM, QK_W), lambda i: (i // tpb, pb + i % tpb, 0))
        vt_shape = (dec_batch, VT_ROWS, past + dec_seq)
        vt_spec = pl.BlockSpec((None, VT_ROWS, TM), lambda i: (i // tpb, 0, pb + i % tpb))
    w_arrs, w_specs = _weights(w, ["n1g", "wcat", "qag", "kvag", "wq", "wkb", "wvt", "aug", "vg", "ws", "bs"])
    in_specs = [pl.BlockSpec((TM, D), row), _mod_spec(tpb)] + w_specs + [pl.BlockSpec((TM, HEAD_PAD), tab_idx)] * 3
    args = [x, mod_l, *w_arrs, *tabs]
    out_shape = [jax.ShapeDtypeStruct((n, QK_W), BF16), jax.ShapeDtypeStruct(k_shape, BF16),
                 jax.ShapeDtypeStruct(vt_shape, BF16), jax.ShapeDtypeStruct((n, GM_W), BF16)]
    out_specs = [pl.BlockSpec((TM, QK_W), row), k_spec, vt_spec, pl.BlockSpec((TM, GM_W), row)]
    aliases = {}
    if is_ctx:
        seqs = TM // CTX_SEQ_LEN
        batch = n // CTX_SEQ_LEN
        n_ab = DEPTH // 2
        out_shape += [jax.ShapeDtypeStruct((batch, n_ab, CTX_SEQ_LEN, KV_RANK), F32),
                      jax.ShapeDtypeStruct((batch, n_ab, CTX_SEQ_LEN, ROPE), F32)]
        out_specs += [pl.BlockSpec((seqs, None, CTX_SEQ_LEN, KV_RANK), lambda i: (i, layer, 0, 0)),
                      pl.BlockSpec((seqs, None, CTX_SEQ_LEN, ROPE), lambda i: (i, layer, 0, 0))]
        if ckv_prev is not None:
            in_specs += [pl.BlockSpec(memory_space=pl.ANY), pl.BlockSpec(memory_space=pl.ANY)]
            args += [ckv_prev, kpe_prev]
            aliases = {len(args) - 2: 4, len(args) - 1: 5}

    def body(*refs):
        n_in = 16
        _in_kernel(*refs[:n_in], *refs[len(args):], is_ctx=is_ctx)

    return pl.pallas_call(
        body,
        grid=(n // TM,),
        in_specs=in_specs,
        out_specs=out_specs,
        out_shape=out_shape,
        input_output_aliases=aliases,
        compiler_params=_params(1),
        name="in_proj_ctx" if is_ctx else "in_proj_lat",
    )(*args)


def _cache_kernel(ckv_ref, kpe_ref, wkb_ref, wvt_ref, aug_ref, place_ref, k_in, vt_in, k_ref, vt_ref):
    del k_in, vt_in
    ckv_b = ckv_ref[...].astype(BF16)
    kn = jnp.dot(ckv_b, wkb_ref[...], preferred_element_type=F32)
    kp = jnp.dot(kpe_ref[...].astype(BF16), place_ref[...], preferred_element_type=F32)
    k_ref[...] = (kn + jnp.concatenate([kp] * HEADS, axis=-1)).astype(BF16)
    vt = lax.dot_general(wvt_ref[...], ckv_b, (((1,), (1,)), ((), ())), preferred_element_type=F32)
    vt_ref[...] = (vt + aug_ref[...]).astype(BF16)


def _cache_keys(cache_ckv, cache_kpe, w, place, k_lat, vt_lat, *, layer):
    dec_batch, _, past, _ = cache_ckv.shape
    w_arrs, w_specs = _weights(dict(w, place=place), ["wkb", "wvt", "aug", "place"])
    return pl.pallas_call(
        _cache_kernel,
        grid=(dec_batch,),
        in_specs=[
            pl.BlockSpec((None, None, past, KV_RANK), lambda b: (b, layer, 0, 0)),
            pl.BlockSpec((None, None, past, ROPE), lambda b: (b, layer, 0, 0)),
            *w_specs,
            pl.BlockSpec(memory_space=pl.ANY),
            pl.BlockSpec(memory_space=pl.ANY),
        ],
        out_specs=[pl.BlockSpec((None, past, QK_W), lambda b: (b, 0, 0)),
                   pl.BlockSpec((None, VT_ROWS, past), lambda b: (b, 0, 0))],
        out_shape=(jax.ShapeDtypeStruct(k_lat.shape, BF16), jax.ShapeDtypeStruct(vt_lat.shape, BF16)),
        input_output_aliases={6: 0, 7: 1},
        compiler_params=_params(1),
        name="cache_keys",
    )(cache_ckv, cache_kpe, *w_arrs, k_lat, vt_lat)


def _attn_kernel(q_ref, k_ref, vt_ref, o_ref, st_ref, p_ref, *, key_blk):
    l_len = k_ref.shape[0]
    n_blk = l_len // key_blk
    blocks = [slice(j * key_blk, (j + 1) * key_blk) for j in range(n_blk)]
    items = [(qt, h) for qt in range(q_ref.shape[0] // TQ) for h in range(HEADS)]
    n_items = len(items)
    m = [None] * n_items
    pair = {}
    for step in range(n_items + 2):
        ia, ib, ic = step, step - 1, step - 2
        mrun = None
        ot = None
        for rows in blocks:
            if ia < n_items:
                qt, h = items[ia]
                cols = slice(h * HEAD_PAD, (h + 1) * HEAD_PAD)
                st = lax.dot_general(k_ref[rows, cols], q_ref[qt * TQ:(qt + 1) * TQ, cols],
                                     (((1,), (1,)), ((), ())), preferred_element_type=F32)
                st_ref[ia % 2, rows, :] = st
                blk_max = jnp.max(st.reshape(key_blk // 8, 8, TQ), axis=0)
                mrun = blk_max if mrun is None else jnp.maximum(mrun, blk_max)
            if 0 <= ib < n_items:
                p_ref[ib % 2, rows, :] = jnp.exp2(st_ref[ib % 2, rows, :] - m[ib]).astype(BF16)
            if 0 <= ic < n_items:
                h = items[ic][1]
                part = jnp.dot(vt_ref[h * V_ROWS:(h + 1) * V_ROWS, rows], p_ref[ic % 2, rows, :],
                               preferred_element_type=F32)
                ot = part if ot is None else ot + part
        if ia < n_items:
            m[ia] = jnp.max(mrun, axis=0, keepdims=True)
        if 0 <= ic < n_items:
            qt, h = items[ic]
            pair[h] = ot[0:VDIM] * (1.0 / ot[VDIM:VDIM + 1])
            if h % 2 == 1:
                both = jnp.concatenate([pair.pop(h - 1), pair.pop(h)], axis=0)
                o_ref[qt * TQ:(qt + 1) * TQ, (h // 2) * LANES:(h // 2 + 1) * LANES] = both.T.astype(BF16)


def _attention(q, k, vt, *, batch, t_len):
    l_len = k.shape[1]
    tq = min(t_len, Q_TILES * TQ)
    nq = t_len // tq
    key_blk = math.gcd(l_len, MAX_KEY_BLK)
    if vt.ndim == 3:
        vt_spec = pl.BlockSpec((None, VT_ROWS, l_len), lambda b, qi: (b, 0, 0))
    else:
        vt_spec = pl.BlockSpec((VT_ROWS, l_len), lambda b, qi: (0, b))
    return pl.pallas_call(
        functools.partial(_attn_kernel, key_blk=key_blk),
        grid=(batch, nq),
        scratch_shapes=[pltpu.VMEM((2, l_len, TQ), F32), pltpu.VMEM((2, l_len, TQ), BF16)],
        in_specs=[
            pl.BlockSpec((tq, QK_W), lambda b, qi: (b * nq + qi, 0)),
            pl.BlockSpec((None, l_len, QK_W), lambda b, qi: (b, 0, 0)),
            vt_spec,
        ],
        out_specs=pl.BlockSpec((tq, HEADS * VDIM), lambda b, qi: (b * nq + qi, 0)),
        out_shape=jax.ShapeDtypeStruct((batch * t_len, HEADS * VDIM), BF16),
        compiler_params=_params(2),
        name="attention",
    )(q, k, vt)


def _mlp_tail(x1, mod_ref, n2g_ref, w1_ref, w2_ref, fg_ref, o_ref, final):
    sh2 = mod_ref[:, 3 * D:4 * D]
    sc2 = mod_ref[:, 4 * D:5 * D]
    g2 = mod_ref[:, 5 * D:6 * D]
    h = (_rms(x1, n2g_ref[...]) * (1.0 + sc2) + sh2).astype(BF16)
    acc = None
    for c in range(D_FF // TF):
        a = jnp.maximum(jnp.dot(h, w1_ref[:, c * TF:(c + 1) * TF], preferred_element_type=F32), 0.0)
        part = jnp.dot((a * a).astype(BF16), w2_ref[c * TF:(c + 1) * TF, :], preferred_element_type=F32)
        acc = part if acc is None else acc + part
    x2 = x1 + g2 * acc
    if final:
        x2 = _rms(x2, fg_ref[...])
    o_ref[...] = x2


def _ffn_ab_kernel(x_ref, attn_ref, gm_ref, mod_ref, wo_ref, n2g_ref, w1_ref, w2_ref, fg_ref, o_ref, *, final):
    g1 = mod_ref[:, 2 * D:3 * D]
    mix = (jnp.dot(attn_ref[...], wo_ref[0:HEADS * VDIM, :], preferred_element_type=F32)
           + jnp.dot(gm_ref[...], wo_ref[HEADS * VDIM:, :], preferred_element_type=F32))
    x1 = x_ref[...] + g1 * mix
    _mlp_tail(x1, mod_ref, n2g_ref, w1_ref, w2_ref, fg_ref, o_ref, final)


def _ffn_pool_kernel(x_ref, xp_ref, xn_ref, mod_ref, n1g_ref, band_ref, wp_ref, ps_ref, n2g_ref, w1_ref, w2_ref,
                     fg_ref, o_ref, *, final, tiles_per_seq):
    sh1 = mod_ref[:, 0 * D:1 * D]
    sc1 = mod_ref[:, 1 * D:2 * D]
    g1 = mod_ref[:, 2 * D:3 * D]
    n1g = n1g_ref[...]
    ada = lambda v: _rms(v, n1g) * (1.0 + sc1) + sh1
    x = x_ref[...]
    ht = ada(x)
    zero_halo = jnp.zeros((HALO, D), F32)
    if tiles_per_seq is None:
        seq_len, pos0 = SEG, 0
    else:
        t = pl.program_id(0) % tiles_per_seq
        seq_len, pos0 = tiles_per_seq * TM, t * TM
        tile_prev = jnp.where(t != 0, ada(xp_ref[...]), 0.0)
        tile_next = jnp.where(t != tiles_per_seq - 1, ada(xn_ref[...]), 0.0)
    n_seg = TM // SEG
    seg_rows = []
    for s in range(n_seg):
        seg = ht[s * SEG:(s + 1) * SEG]
        if tiles_per_seq is None:
            prev, nxt, base = zero_halo, zero_halo, 0
        else:
            prev = tile_prev if s == 0 else ht[s * SEG - HALO:s * SEG]
            nxt = tile_next if s == n_seg - 1 else ht[(s + 1) * SEG:(s + 1) * SEG + HALO]
            base = pos0 + s * SEG
        hs = jnp.concatenate([prev, seg, nxt], axis=0)
        hi = hs.astype(BF16)
        lo = (hs - hi.astype(F32)).astype(BF16)
        pos = base + lax.broadcasted_iota(jnp.int32, (SEG, POOL_GD), 0)
        parts = []
        for gi, w in enumerate(POOL_WINDOWS):
            half = w // 2
            cols = slice(gi * POOL_GD, (gi + 1) * POOL_GD)
            top = (jnp.dot(band_ref[gi, 0], hi[0:SEG, cols], preferred_element_type=F32)
                   + jnp.dot(band_ref[gi, 0], lo[0:SEG, cols], preferred_element_type=F32))
            bot = (jnp.dot(band_ref[gi, 1], hi[2 * HALO:, cols], preferred_element_type=F32)
                   + jnp.dot(band_ref[gi, 1], lo[2 * HALO:, cols], preferred_element_type=F32))
            tot = jnp.concatenate([top, bot], axis=0)
            cnt = (jnp.minimum(pos + half, seq_len) - jnp.maximum(pos - half, 0)).astype(F32)
            diff = (tot / cnt - seg[:, cols]).astype(BF16)
            parts.append(jnp.dot(diff, wp_ref[gi], preferred_element_type=F32))
        seg_rows.append(jnp.concatenate(parts, axis=-1))
    mix = jnp.concatenate(seg_rows, axis=0) * ps_ref[...]
    x1 = x + g1 * mix
    _mlp_tail(x1, mod_ref, n2g_ref, w1_ref, w2_ref, fg_ref, o_ref, final)


def _pool_bands():
    half_rows = SEG // 2
    t = np.arange(half_rows)[:, None]
    u = np.arange(SEG)[None, :]
    bands = np.zeros((len(POOL_WINDOWS), 2, half_rows, SEG), np.float32)
    for gi, w in enumerate(POOL_WINDOWS):
        half = w // 2
        d_top = (u - HALO) - t
        d_bot = (u + HALO) - (half_rows + t)
        bands[gi, 0] = (d_top >= -half) & (d_top <= half - 1)
        bands[gi, 1] = (d_bot >= -half) & (d_bot <= half - 1)
    return jnp.asarray(bands, BF16)


def _ffn_ab(x, attn, gm, mod_l, w, final, tiles_per_batch):
    n = x.shape[0]
    row = lambda i: (i, 0)
    w_arrs, w_specs = _weights(w, ["wo", "n2g", "w1", "w2", "fg"])
    return pl.pallas_call(
        functools.partial(_ffn_ab_kernel, final=final),
        grid=(n // TM,),
        in_specs=[
            pl.BlockSpec((TM, D), row),
            pl.BlockSpec((TM, HEADS * VDIM), row),
            pl.BlockSpec((TM, GM_W), row),
            _mod_spec(tiles_per_batch),
        ] + w_specs,
        out_specs=pl.BlockSpec((TM, D), row),
        out_shape=jax.ShapeDtypeStruct((n, D), F32),
        compiler_params=_params(1),
        name="ffn_ab",
    )(x, attn, gm, mod_l, *w_arrs)


def _ffn_pool(x, mod_l, w, final, tiles_per_batch):
    n = x.shape[0]
    row = lambda i: (i, 0)
    hb = TM // HALO
    w_arrs, w_specs = _weights(w, ["n1g", "bands", "wp", "ps", "n2g", "w1", "w2", "fg"])
    return pl.pallas_call(
        functools.partial(_ffn_pool_kernel, final=final, tiles_per_seq=tiles_per_batch),
        grid=(n // TM,),
        in_specs=[
            pl.BlockSpec((TM, D), row),
            pl.BlockSpec((HALO, D), lambda i: (jnp.maximum(i * hb - 1, 0), 0)),
            pl.BlockSpec((HALO, D), lambda i: (jnp.minimum((i + 1) * hb, n // HALO - 1), 0)),
            _mod_spec(tiles_per_batch),
        ] + w_specs,
        out_specs=pl.BlockSpec((TM, D), row),
        out_shape=jax.ShapeDtypeStruct((n, D), F32),
        compiler_params=_params(1),
        name="ffn_pool",
    )(x, x, x, mod_l, *w_arrs)


_SWAP = np.arange(ROPE).reshape(2, 2, ROPE // 4)[:, ::-1, :].reshape(ROPE)


def _rope_tables(t_len):
    rows = t_len // GRID_W
    row = jnp.repeat(jnp.arange(rows), GRID_W).astype(F32)
    col = jnp.tile(jnp.arange(GRID_W), rows).astype(F32)
    axis_dim = ROPE // 2
    inv = ROPE_BASE ** (-jnp.arange(0, axis_dim, 2, dtype=F32) / axis_dim)
    ang = jnp.stack([row[:, None] * inv, col[:, None] * inv], axis=1)
    cos = jnp.cos(ang)
    sin = jnp.sin(ang)
    c32 = jnp.stack([cos, cos], axis=2).reshape(t_len, ROPE)
    s32 = jnp.stack([-sin, sin], axis=2).reshape(t_len, ROPE)
    one = lambda r, c: jnp.ones((r, c), F32)
    zero = lambda r, c: jnp.zeros((r, c), F32)
    tq = jnp.concatenate([jnp.concatenate([one(t_len, NOPE), c32, s32], -1),
                          jnp.concatenate([one(TM, NOPE + ROPE), zero(TM, ROPE)], -1)], 0)
    ck = jnp.concatenate([jnp.concatenate([zero(t_len, NOPE), c32, c32], -1),
                          jnp.concatenate([zero(TM, NOPE), one(TM, 2 * ROPE)], -1)], 0)
    sk = jnp.concatenate([jnp.concatenate([zero(t_len, NOPE), s32, s32], -1), zero(TM, HEAD_PAD)], 0)
    return tq, ck, sk


def _prep_even_weights(w_in_ab, q_a_g, kv_a_g, w_q_b, w_kv_b, gmlp_v_g, w_spatial, b_spatial, w_out_ab):
    n_ab = w_in_ab.shape[0]
    kpe_off = Q_RANK + KV_RANK
    kp_w = w_in_ab[:, :, kpe_off:kpe_off + ROPE]
    kp_sw = kp_w[:, :, _SWAP]
    z = lambda c: jnp.zeros((n_ab, D, c), F32)
    wcat = jnp.concatenate(
        [w_in_ab[:, :, :kpe_off], w_in_ab[:, :, kpe_off + ROPE:],
         z(NOPE), kp_w, kp_w, z(NOPE), kp_sw, kp_sw, kp_w, z(HEAD_PAD - ROPE)], axis=2).astype(BF16)
    wq4 = w_q_b.reshape(n_ab, Q_RANK, HEADS, NOPE + ROPE)
    wq = jnp.concatenate([wq4, wq4[..., NOPE:][..., _SWAP]], axis=-1).reshape(n_ab, Q_RANK, QK_W).astype(BF16)
    wkv4 = w_kv_b.reshape(n_ab, KV_RANK, HEADS, NOPE + VDIM)
    wkb = jnp.pad(wkv4[..., :NOPE], ((0, 0), (0, 0), (0, 0), (0, HEAD_PAD - NOPE))).reshape(
        n_ab, KV_RANK, QK_W).astype(BF16)
    wvt = jnp.pad(wkv4[..., NOPE:].transpose(0, 2, 3, 1), ((0, 0), (0, 0), (0, V_ROWS - VDIM), (0, 0))).reshape(
        n_ab, VT_ROWS, KV_RANK).astype(BF16)
    ws = w_spatial.reshape(n_ab, GM_GROUPS // 2, 2, GM_CHUNK, GM_CHUNK).transpose(0, 1, 3, 2, 4).reshape(
        n_ab, GM_GROUPS // 2, GM_CHUNK, 2 * GM_CHUNK).astype(BF16)
    bs = jnp.repeat(b_spatial.transpose(0, 2, 1), GM_W // GM_GROUPS, axis=2)
    aug = np.zeros((VT_ROWS, 1), np.float32)
    aug[VDIM::V_ROWS] = 1.0
    return dict(wcat=wcat, wq=wq, wkb=wkb, wvt=wvt, ws=ws, bs=bs, aug=jnp.asarray(aug),
                qag=q_a_g[:, None], kvag=kv_a_g[:, None], vg=gmlp_v_g[:, None], wo=w_out_ab.astype(BF16))


def kernel(x_prompt, x_sample, cache_ckv, cache_kpe, c, c_ctx, w_mod, b_mod, norm1_g, norm2_g, w_in_ab, q_a_g,
           kv_a_g, w_q_b, w_kv_b, gmlp_v_g, w_spatial, b_spatial, w_out_ab, w_pool, pool_scale, w_ff1, w_ff2,
           final_g):
    batch, seq, _ = x_prompt.shape
    dec_batch, dec_seq, _ = x_sample.shape
    past = cache_ckv.shape[2]
    assert seq == CTX_SEQ_LEN and dec_seq % TM == 0 and past % TM == 0
    lat_tiles = dec_seq // TM

    xc = x_prompt.reshape(batch * seq, D)
    xl = x_sample.reshape(dec_batch * dec_seq, D)
    cond8 = jnp.concatenate([c_ctx[None], c, jnp.zeros((MOD_ROWS - 1 - dec_batch, D), F32)], axis=0)
    mod = _modulation(cond8, w_mod, b_mod).reshape(DEPTH, MOD_ROWS, 1, N_MOD * D)
    tabs = _rope_tables(dec_seq)
    bands = _pool_bands()
    place = np.zeros((ROPE, HEAD_PAD), np.float32)
    place[np.arange(ROPE), NOPE + np.arange(ROPE)] = 1.0
    place[np.arange(ROPE), NOPE + ROPE + np.arange(ROPE)] = 1.0
    place = jnp.asarray(place, BF16)

    ev = _prep_even_weights(w_in_ab, q_a_g, kv_a_g, w_q_b, w_kv_b, gmlp_v_g, w_spatial, b_spatial, w_out_ab)
    w1 = w_ff1.astype(BF16)
    w2 = w_ff2.astype(BF16)
    wp = w_pool.astype(BF16)
    n1g = norm1_g[:, None]
    n2g = norm2_g[:, None]
    ps = pool_scale[:, None]

    new_ckv = new_kpe = None
    for l in range(DEPTH):
        final = l == DEPTH - 1
        i = l // 2
        w = dict(n1g=(n1g, l), n2g=(n2g, l), w1=(w1, l), w2=(w2, l), fg=final_g[None], bands=bands)
        if l % 2 == 0:
            w.update({name: (val, i) for name, val in ev.items() if name != "aug"}, aug=ev["aug"])
            qc, kc, vtc, gmc, new_ckv, new_kpe = _in_proj(
                xc, mod[l], w, tabs, layer=i, ckv_prev=new_ckv, kpe_prev=new_kpe)
            ql, kl, vtl, gml = _in_proj(xl, mod[l], w, tabs, layer=i, lat_shape=(dec_batch, dec_seq, past))
            kl, vtl = _cache_keys(cache_ckv, cache_kpe, w, place, kl, vtl, layer=i)
            attn_c = _attention(qc, kc.reshape(batch, seq, QK_W), vtc, batch=batch, t_len=seq)
            attn_l = _attention(ql, kl, vtl, batch=dec_batch, t_len=dec_seq)
            xc = _ffn_ab(xc, attn_c, gmc, mod[l], w, final, None)
            xl = _ffn_ab(xl, attn_l, gml, mod[l], w, final, lat_tiles)
        else:
            w.update(wp=(wp, i), ps=(ps, i))
            xc = _ffn_pool(xc, mod[l], w, final, None)
            xl = _ffn_pool(xl, mod[l], w, final, lat_tiles)

    return xc.reshape(batch, seq, D), xl.reshape(dec_batch, dec_seq, D), new_ckv, new_kpe
```

```python
import functools
import math

import numpy as np
import jax
import jax.numpy as jnp
from jax import lax
from jax.experimental import pallas as pl
from jax.experimental.pallas import tpu as pltpu

F32 = jnp.float32
BF16 = jnp.bfloat16

D = 1024
DEPTH = 4
N_MOD = 6
HEADS = 8
NOPE = 64
ROPE = 32
VDIM = 64
Q_RANK = 384
KV_RANK = 256
GRID_W = 64
ROPE_BASE = 10000.0
ATTN_SCALE = (NOPE + ROPE) ** -0.5
GM_W = 512
GM_GROUPS = 8
GM_CHUNK = 128
POOL_WINDOWS = (2, 4, 8, 16)
POOL_GD = D // len(POOL_WINDOWS)
D_FF = 4 * D
EPS = 1e-6
CTX_SEQ_LEN = 256

LANES = 128
HEAD_PAD = LANES
QK_W = HEADS * HEAD_PAD
V_ROWS = VDIM + 16
VT_ROWS = HEADS * V_ROWS
MOD_ROWS = 8
HALO = 8
SEG = 256

C_QA = 0
C_KVA = C_QA + Q_RANK
C_U = C_KVA + KV_RANK
C_V = C_U + GM_W
C_KP = C_V + GM_W
C_KPS = C_KP + HEAD_PAD
C_KO = C_KPS + HEAD_PAD
IN_W = C_KO + HEAD_PAD

TM = 1024
TM_F = 1024
TF = 1024
TQ = 256
Q_TILES = 2
MAX_KEY_BLK = 2048
VMEM_LIMIT = 56 * 1024 * 1024


def _rms(x, g):
    return x * lax.rsqrt(jnp.mean(x * x, axis=-1, keepdims=True) + EPS) * g


def _const_spec(shape):
    nd = len(shape)
    return pl.BlockSpec(shape, lambda *_: (0,) * nd)


def _weights(w, names):
    arrs, specs = [], []
    for name in names:
        entry = w[name]
        if isinstance(entry, tuple):
            arr, layer = entry
            shape = arr.shape[1:]
            specs.append(pl.BlockSpec((None,) + shape, lambda *_, layer=layer, nd=len(shape): (layer,) + (0,) * nd,
                                      pipeline_mode=pl.Buffered(1)))
        else:
            arr = entry
            specs.append(pl.BlockSpec(arr.shape, lambda *_, nd=arr.ndim: (0,) * nd, pipeline_mode=pl.Buffered(1)))
        arrs.append(arr)
    return arrs, specs


def _params(n_axes, flags=None):
    return pltpu.CompilerParams(dimension_semantics=("arbitrary",) * n_axes, vmem_limit_bytes=VMEM_LIMIT,
                                flags=flags)


def _mod_kernel(cond_ref, w_ref, b_ref, o_ref):
    c = cond_ref[...]
    s = (c / (1.0 + jnp.exp(-c))).astype(BF16)
    w = w_ref[...].astype(BF16)
    o_ref[...] = jnp.dot(s, w, preferred_element_type=F32) + b_ref[...]


def _modulation(cond8, w_mod, b_mod):
    tn = 1536
    nw = N_MOD * D
    return pl.pallas_call(
        _mod_kernel,
        grid=(DEPTH, nw // tn),
        in_specs=[
            pl.BlockSpec((MOD_ROWS, D), lambda l, j: (0, 0)),
            pl.BlockSpec((None, D, tn), lambda l, j: (l, 0, j)),
            pl.BlockSpec((None, 1, tn), lambda l, j: (l, 0, j)),
        ],
        out_specs=pl.BlockSpec((None, MOD_ROWS, tn), lambda l, j: (l, 0, j)),
        out_shape=jax.ShapeDtypeStruct((DEPTH, MOD_ROWS, nw), F32),
        compiler_params=_params(2),
        name="modulation",
    )(cond8, w_mod, b_mod.reshape(DEPTH, 1, nw))


def _mod_spec(tiles_per_batch):
    if tiles_per_batch is None:
        return pl.BlockSpec((None, 1, N_MOD * D), lambda i: (0, 0, 0))
    return pl.BlockSpec((None, 1, N_MOD * D), lambda i: (1 + i // tiles_per_batch, 0, 0))


def _in_kernel(x_ref, mod_ref, n1g_ref, wcat_ref, qag_ref, kvag_ref, wq_ref, wkb_ref, wvt_ref, aug_ref,
               vg_ref, ws_ref, bs_ref, tq_ref, ck_ref, sk_ref, *rest, is_ctx, layer, n_prev):
    prev_refs, out_refs = rest[:n_prev], rest[n_prev:]
    if is_ctx:
        q_ref, k_ref, vt_ref, gm_ref, ckv_ref, kpe_ref = out_refs
    else:
        q_ref, k_ref, vt_ref, gm_ref = out_refs
    tm = x_ref.shape[0]
    x = x_ref[...]
    sh1 = mod_ref[:, 0 * D:1 * D]
    sc1 = mod_ref[:, 1 * D:2 * D]
    h = (_rms(x, n1g_ref[...]) * (1.0 + sc1) + sh1).astype(BF16)
    z = jnp.dot(h, wcat_ref[...], preferred_element_type=F32)

    qn = _rms(z[:, C_QA:C_QA + Q_RANK], qag_ref[...]).astype(BF16)
    qa = jnp.dot(qn, wq_ref[...], preferred_element_type=F32)
    tq8 = jnp.concatenate([tq_ref[...]] * HEADS, axis=-1)
    q_ref[...] = (qa * tq8 * (ATTN_SCALE * math.log2(math.e))).astype(BF16)

    ckv = _rms(z[:, C_KVA:C_KVA + KV_RANK], kvag_ref[...])
    kp_rot = z[:, C_KP:C_KP + HEAD_PAD] * ck_ref[...] + z[:, C_KPS:C_KPS + HEAD_PAD] * sk_ref[...]
    ckv_b = ckv.astype(BF16)
    kn = jnp.dot(ckv_b, wkb_ref[...], preferred_element_type=F32)
    k_ref[...] = (kn + jnp.concatenate([kp_rot] * HEADS, axis=-1)).astype(BF16)
    vt = lax.dot_general(wvt_ref[...], ckv_b, (((1,), (1,)), ((), ())), preferred_element_type=F32)
    vt_ref[...] = (vt + aug_ref[...]).astype(BF16)
    if is_ctx:
        seqs, n_ab = ckv_ref.shape[0], ckv_ref.shape[1]
        for slot in range(n_ab):
            if slot == layer:
                ckv_ref[:, slot] = ckv.reshape(seqs, CTX_SEQ_LEN, KV_RANK)
                kpe_ref[:, slot] = z[:, C_KO:C_KO + ROPE].reshape(seqs, CTX_SEQ_LEN, ROPE)
            elif prev_refs:
                ckv_ref[:, slot] = prev_refs[0][:, slot]
                kpe_ref[:, slot] = prev_refs[1][:, slot]
            else:
                ckv_ref[:, slot] = jnp.zeros((seqs, CTX_SEQ_LEN, KV_RANK), F32)
                kpe_ref[:, slot] = jnp.zeros((seqs, CTX_SEQ_LEN, ROPE), F32)

    u = jax.nn.gelu(z[:, C_U:C_U + GM_W])
    vv = jax.nn.gelu(z[:, C_V:C_V + GM_W])
    vn = _rms(vv, vg_ref[...])
    lane = lax.broadcasted_iota(jnp.int32, (GM_CHUNK, LANES), 1)
    lo = lane < GM_W // GM_GROUPS
    for c in range(tm // GM_CHUNK):
        rows = slice(c * GM_CHUNK, (c + 1) * GM_CHUNK)
        for j in range(GM_W // LANES):
            cols = slice(j * LANES, (j + 1) * LANES)
            blk = vn[rows, cols]
            rhs = jnp.concatenate([jnp.where(lo, blk, 0.0), jnp.where(lo, 0.0, blk)], axis=0).astype(BF16)
            s = jnp.dot(ws_ref[j], rhs, preferred_element_type=F32) + bs_ref[:, cols]
            gm_ref[rows, cols] = (u[rows, cols] * s).astype(BF16)


def _in_proj(x, mod_l, w, tabs, *, layer, ckv_prev=None, kpe_prev=None, lat_shape=None):
    n = x.shape[0]
    is_ctx = lat_shape is None
    row = lambda i: (i, 0)
    id_blk = tabs[0].shape[0] // TM - 1
    if is_ctx:
        tpb = None
        tab_idx = lambda i: (id_blk, 0)
        vt_shape, vt_spec = (VT_ROWS, n), pl.BlockSpec((VT_ROWS, TM), lambda i: (0, i))
    else:
        dec_batch, dec_seq = lat_shape
        tpb = dec_seq // TM
        tab_idx = lambda i: (i % tpb, 0)
        vt_shape = (dec_batch, VT_ROWS, dec_seq)
        vt_spec = pl.BlockSpec((None, VT_ROWS, TM), lambda i: (i // tpb, 0, i % tpb))
    w_arrs, w_specs = _weights(w, ["n1g", "wcat", "qag", "kvag", "wq", "wkb", "wvt", "aug", "vg", "ws", "bs"])
    in_specs = [pl.BlockSpec((TM, D), row), _mod_spec(tpb)] + w_specs + [pl.BlockSpec((TM, HEAD_PAD), tab_idx)] * 3
    args = [x, mod_l, *w_arrs, *tabs]
    out_shape = [jax.ShapeDtypeStruct((n, QK_W), BF16), jax.ShapeDtypeStruct((n, QK_W), BF16),
                 jax.ShapeDtypeStruct(vt_shape, BF16), jax.ShapeDtypeStruct((n, GM_W), BF16)]
    out_specs = [pl.BlockSpec((TM, QK_W), row), pl.BlockSpec((TM, QK_W), row), vt_spec,
                 pl.BlockSpec((TM, GM_W), row)]
    aliases = {}
    n_prev = 0
    if is_ctx:
        seqs = TM // CTX_SEQ_LEN
        batch = n // CTX_SEQ_LEN
        n_ab = DEPTH // 2
        cache_specs = [pl.BlockSpec((seqs, n_ab, CTX_SEQ_LEN, KV_RANK), lambda i: (i, 0, 0, 0)),
                       pl.BlockSpec((seqs, n_ab, CTX_SEQ_LEN, ROPE), lambda i: (i, 0, 0, 0))]
        out_shape += [jax.ShapeDtypeStruct((batch, n_ab, CTX_SEQ_LEN, KV_RANK), F32),
                      jax.ShapeDtypeStruct((batch, n_ab, CTX_SEQ_LEN, ROPE), F32)]
        out_specs += cache_specs
        if ckv_prev is not None:
            in_specs += cache_specs
            args += [ckv_prev, kpe_prev]
            aliases = {len(args) - 2: 4, len(args) - 1: 5}
            n_prev = 2

    return pl.pallas_call(
        functools.partial(_in_kernel, is_ctx=is_ctx, layer=layer, n_prev=n_prev),
        grid=(n // TM,),
        in_specs=in_specs,
        out_specs=out_specs,
        out_shape=out_shape,
        input_output_aliases=aliases,
        compiler_params=_params(1),
        name="in_proj_ctx" if is_ctx else "in_proj_lat",
    )(*args)


def _cache_kernel(ckv_ref, kpe_ref, wkb_ref, wvt_ref, aug_ref, place_ref, k_ref, vt_ref):
    ckv_b = ckv_ref[...].astype(BF16)
    kn = jnp.dot(ckv_b, wkb_ref[...], preferred_element_type=F32)
    kp = jnp.dot(kpe_ref[...].astype(BF16), place_ref[...], preferred_element_type=F32)
    k_ref[...] = (kn + jnp.concatenate([kp] * HEADS, axis=-1)).astype(BF16)
    vt = lax.dot_general(wvt_ref[...], ckv_b, (((1,), (1,)), ((), ())), preferred_element_type=F32)
    vt_ref[...] = (vt + aug_ref[...]).astype(BF16)


def _cache_keys(cache_ckv, cache_kpe, w, place, *, layer):
    dec_batch, _, past, _ = cache_ckv.shape
    w_arrs, w_specs = _weights(dict(w, place=place), ["wkb", "wvt", "aug", "place"])
    return pl.pallas_call(
        _cache_kernel,
        grid=(dec_batch,),
        in_specs=[
            pl.BlockSpec((None, None, past, KV_RANK), lambda b: (b, layer, 0, 0)),
            pl.BlockSpec((None, None, past, ROPE), lambda b: (b, layer, 0, 0)),
            *w_specs,
        ],
        out_specs=[pl.BlockSpec((None, past, QK_W), lambda b: (b, 0, 0)),
                   pl.BlockSpec((None, VT_ROWS, past), lambda b: (b, 0, 0))],
        out_shape=(jax.ShapeDtypeStruct((dec_batch, past, QK_W), BF16),
                   jax.ShapeDtypeStruct((dec_batch, VT_ROWS, past), BF16)),
        compiler_params=_params(1),
        name="cache_keys",
    )(cache_ckv, cache_kpe, *w_arrs)


def _attn_kernel(q_ref, *refs, n_src):
    srcs = [(refs[2 * s], refs[2 * s + 1]) for s in range(n_src)]
    o_ref, st_ref, p_ref = refs[2 * n_src:]
    blocks = []
    base = 0
    for k_ref, vt_ref in srcs:
        l_src = k_ref.shape[0]
        for start in range(0, l_src, MAX_KEY_BLK):
            size = min(MAX_KEY_BLK, l_src - start)
            blocks.append((k_ref, vt_ref, slice(start, start + size), slice(base + start, base + start + size)))
        base += l_src
    n_st = st_ref.shape[0]
    items = [(qt, h) for qt in range(q_ref.shape[0] // TQ) for h in range(HEADS)]
    n_items = len(items)
    m = [None] * n_items
    pair = {}
    for step in range(n_items + 2):
        ia, ib, ic = step, step - 1, step - 2
        mrun = None
        ot = None
        for k_ref, vt_ref, src_rows, rows in blocks:
            if ia < n_items:
                qt, h = items[ia]
                cols = slice(h * HEAD_PAD, (h + 1) * HEAD_PAD)
                st = lax.dot_general(k_ref[src_rows, cols], q_ref[qt * TQ:(qt + 1) * TQ, cols],
                                     (((1,), (1,)), ((), ())), preferred_element_type=F32)
                st_ref[ia % n_st, rows, :] = st
                blk_max = jnp.max(st.reshape(st.shape[0] // 8, 8, TQ), axis=0)
                mrun = blk_max if mrun is None else jnp.maximum(mrun, blk_max)
            if 0 <= ib < n_items:
                p_ref[ib % 2, rows, :] = jnp.exp2(st_ref[ib % n_st, rows, :] - m[ib]).astype(BF16)
            if 0 <= ic < n_items:
                h = items[ic][1]
                part = jnp.dot(vt_ref[h * V_ROWS:(h + 1) * V_ROWS, src_rows], p_ref[ic % 2, rows, :],
                               preferred_element_type=F32)
                ot = part if ot is None else ot + part
        if ia < n_items:
            m[ia] = jnp.max(mrun, axis=0, keepdims=True)
        if 0 <= ic < n_items:
            qt, h = items[ic]
            pair[h] = ot[0:VDIM] * (1.0 / ot[VDIM:VDIM + 1])
            if h % 2 == 1:
                both = jnp.concatenate([pair.pop(h - 1), pair.pop(h)], axis=0)
                o_ref[qt * TQ:(qt + 1) * TQ, (h // 2) * LANES:(h // 2 + 1) * LANES] = both.T.astype(BF16)


def _attention(q, sources, *, batch, t_len):
    tq = min(t_len, Q_TILES * TQ)
    nq = t_len // tq
    in_specs = [pl.BlockSpec((tq, QK_W), lambda b, qi: (b * nq + qi, 0))]
    args = [q]
    l_total = 0
    for k, vt in sources:
        l_len = k.shape[1]
        l_total += l_len
        in_specs.append(pl.BlockSpec((None, l_len, QK_W), lambda b, qi: (b, 0, 0)))
        if vt.ndim == 3:
            in_specs.append(pl.BlockSpec((None, VT_ROWS, l_len), lambda b, qi: (b, 0, 0)))
        else:
            in_specs.append(pl.BlockSpec((VT_ROWS, l_len), lambda b, qi: (0, b)))
        args += [k, vt]
    return pl.pallas_call(
        functools.partial(_attn_kernel, n_src=len(sources)),
        grid=(batch, nq),
        scratch_shapes=[pltpu.VMEM((2, l_total, TQ), F32), pltpu.VMEM((2, l_total, TQ), BF16)],
        in_specs=in_specs,
        out_specs=pl.BlockSpec((tq, HEADS * VDIM), lambda b, qi: (b * nq + qi, 0)),
        out_shape=jax.ShapeDtypeStruct((batch * t_len, HEADS * VDIM), BF16),
        compiler_params=_params(2),
        name="attention",
    )(*args)


def _mlp_tail(x1, mod_ref, n2g_ref, w1_ref, w2_ref, fg_ref, o_ref, final):
    sh2 = mod_ref[:, 3 * D:4 * D]
    sc2 = mod_ref[:, 4 * D:5 * D]
    g2 = mod_ref[:, 5 * D:6 * D]
    h = (_rms(x1, n2g_ref[...]) * (1.0 + sc2) + sh2).astype(BF16)
    acc = None
    for c in range(D_FF // TF):
        a = jnp.maximum(jnp.dot(h, w1_ref[:, c * TF:(c + 1) * TF], preferred_element_type=F32), 0.0)
        part = jnp.dot((a * a).astype(BF16), w2_ref[c * TF:(c + 1) * TF, :], preferred_element_type=F32)
        acc = part if acc is None else acc + part
    x2 = x1 + g2 * acc
    if final:
        x2 = _rms(x2, fg_ref[...])
    o_ref[...] = x2


def _ffn_ab_kernel(x_ref, attn_ref, gm_ref, mod_ref, wo_ref, n2g_ref, w1_ref, w2_ref, fg_ref, o_ref, *, final):
    g1 = mod_ref[:, 2 * D:3 * D]
    mix = (jnp.dot(attn_ref[...], wo_ref[0:HEADS * VDIM, :], preferred_element_type=F32)
           + jnp.dot(gm_ref[...], wo_ref[HEADS * VDIM:, :], preferred_element_type=F32))
    x1 = x_ref[...] + g1 * mix
    _mlp_tail(x1, mod_ref, n2g_ref, w1_ref, w2_ref, fg_ref, o_ref, final)


def _ffn_pool_kernel(x_ref, xp_ref, xn_ref, mod_ref, n1g_ref, band_ref, wp_ref, ps_ref, n2g_ref, w1_ref, w2_ref,
                     fg_ref, o_ref, *, final, tiles_per_seq):
    sh1 = mod_ref[:, 0 * D:1 * D]
    sc1 = mod_ref[:, 1 * D:2 * D]
    g1 = mod_ref[:, 2 * D:3 * D]
    n1g = n1g_ref[...]
    ada = lambda v: _rms(v, n1g) * (1.0 + sc1) + sh1
    x = x_ref[...]
    ht = ada(x)
    zero_halo = jnp.zeros((HALO, D), F32)
    if tiles_per_seq is None:
        seq_len, pos0 = SEG, 0
    else:
        t = pl.program_id(0) % tiles_per_seq
        seq_len, pos0 = tiles_per_seq * TM_F, t * TM_F
        tile_prev = jnp.where(t != 0, ada(xp_ref[...]), 0.0)
        tile_next = jnp.where(t != tiles_per_seq - 1, ada(xn_ref[...]), 0.0)
    n_seg = TM_F // SEG
    seg_rows = []
    for s in range(n_seg):
        seg = ht[s * SEG:(s + 1) * SEG]
        if tiles_per_seq is None:
            prev, nxt, base = zero_halo, zero_halo, 0
        else:
            prev = tile_prev if s == 0 else ht[s * SEG - HALO:s * SEG]
            nxt = tile_next if s == n_seg - 1 else ht[(s + 1) * SEG:(s + 1) * SEG + HALO]
            base = pos0 + s * SEG
        hs = jnp.concatenate([prev, seg, nxt], axis=0)
        hi = hs.astype(BF16)
        lo = (hs - hi.astype(F32)).astype(BF16)
        pos = base + lax.broadcasted_iota(jnp.int32, (SEG, POOL_GD), 0)
        parts = []
        for gi, w in enumerate(POOL_WINDOWS):
            half = w // 2
            cols = slice(gi * POOL_GD, (gi + 1) * POOL_GD)
            top = (jnp.dot(band_ref[gi, 0], hi[0:SEG, cols], preferred_element_type=F32)
                   + jnp.dot(band_ref[gi, 0], lo[0:SEG, cols], preferred_element_type=F32))
            bot = (jnp.dot(band_ref[gi, 1], hi[2 * HALO:, cols], preferred_element_type=F32)
                   + jnp.dot(band_ref[gi, 1], lo[2 * HALO:, cols], preferred_element_type=F32))
            tot = jnp.concatenate([top, bot], axis=0)
            cnt = (jnp.minimum(pos + half, seq_len) - jnp.maximum(pos - half, 0)).astype(F32)
            diff = (tot / cnt - seg[:, cols]).astype(BF16)
            parts.append(jnp.dot(diff, wp_ref[gi], preferred_element_type=F32))
        seg_rows.append(jnp.concatenate(parts, axis=-1))
    mix = jnp.concatenate(seg_rows, axis=0) * ps_ref[...]
    x1 = x + g1 * mix
    _mlp_tail(x1, mod_ref, n2g_ref, w1_ref, w2_ref, fg_ref, o_ref, final)


def _pool_bands():
    half_rows = SEG // 2
    t = np.arange(half_rows)[:, None]
    u = np.arange(SEG)[None, :]
    bands = np.zeros((len(POOL_WINDOWS), 2, half_rows, SEG), np.float32)
    for gi, w in enumerate(POOL_WINDOWS):
        half = w // 2
        d_top = (u - HALO) - t
        d_bot = (u + HALO) - (half_rows + t)
        bands[gi, 0] = (d_top >= -half) & (d_top <= half - 1)
        bands[gi, 1] = (d_bot >= -half) & (d_bot <= half - 1)
    return jnp.asarray(bands, BF16)


def _ffn_ab(x, attn, gm, mod_l, w, final, tiles_per_batch):
    n = x.shape[0]
    row = lambda i: (i, 0)
    w_arrs, w_specs = _weights(w, ["wo", "n2g", "w1", "w2", "fg"])
    return pl.pallas_call(
        functools.partial(_ffn_ab_kernel, final=final),
        grid=(n // TM_F,),
        in_specs=[
            pl.BlockSpec((TM_F, D), row),
            pl.BlockSpec((TM_F, HEADS * VDIM), row),
            pl.BlockSpec((TM_F, GM_W), row),
            _mod_spec(tiles_per_batch),
        ] + w_specs,
        out_specs=pl.BlockSpec((TM_F, D), row),
        out_shape=jax.ShapeDtypeStruct((n, D), F32),
        compiler_params=_params(1),
        name="ffn_ab",
    )(x, attn, gm, mod_l, *w_arrs)


def _ffn_pool(x, mod_l, w, final, tiles_per_batch):
    n = x.shape[0]
    row = lambda i: (i, 0)
    hb = TM_F // HALO
    w_arrs, w_specs = _weights(w, ["n1g", "bands", "wp", "ps", "n2g", "w1", "w2", "fg"])
    return pl.pallas_call(
        functools.partial(_ffn_pool_kernel, final=final, tiles_per_seq=tiles_per_batch),
        grid=(n // TM_F,),
        in_specs=[
            pl.BlockSpec((TM_F, D), row),
            pl.BlockSpec((HALO, D), lambda i: (jnp.maximum(i * hb - 1, 0), 0)),
            pl.BlockSpec((HALO, D), lambda i: (jnp.minimum((i + 1) * hb, n // HALO - 1), 0)),
            _mod_spec(tiles_per_batch),
        ] + w_specs,
        out_specs=pl.BlockSpec((TM_F, D), row),
        out_shape=jax.ShapeDtypeStruct((n, D), F32),
        compiler_params=_params(1),
        name="ffn_pool",
    )(x, x, x, mod_l, *w_arrs)


_SWAP = np.arange(ROPE).reshape(2, 2, ROPE // 4)[:, ::-1, :].reshape(ROPE)


def _rope_tables(t_len):
    rows = t_len // GRID_W
    row = jnp.repeat(jnp.arange(rows), GRID_W).astype(F32)
    col = jnp.tile(jnp.arange(GRID_W), rows).astype(F32)
    axis_dim = ROPE // 2
    inv = ROPE_BASE ** (-jnp.arange(0, axis_dim, 2, dtype=F32) / axis_dim)
    ang = jnp.stack([row[:, None] * inv, col[:, None] * inv], axis=1)
    cos = jnp.cos(ang)
    sin = jnp.sin(ang)
    c32 = jnp.stack([cos, cos], axis=2).reshape(t_len, ROPE)
    s32 = jnp.stack([-sin, sin], axis=2).reshape(t_len, ROPE)
    one = lambda r, c: jnp.ones((r, c), F32)
    zero = lambda r, c: jnp.zeros((r, c), F32)
    tq = jnp.concatenate([jnp.concatenate([one(t_len, NOPE), c32, s32], -1),
                          jnp.concatenate([one(TM, NOPE + ROPE), zero(TM, ROPE)], -1)], 0)
    ck = jnp.concatenate([jnp.concatenate([zero(t_len, NOPE), c32, c32], -1),
                          jnp.concatenate([zero(TM, NOPE), one(TM, 2 * ROPE)], -1)], 0)
    sk = jnp.concatenate([jnp.concatenate([zero(t_len, NOPE), s32, s32], -1), zero(TM, HEAD_PAD)], 0)
    return tq, ck, sk


def _prep_even_weights(w_in_ab, q_a_g, kv_a_g, w_q_b, w_kv_b, gmlp_v_g, w_spatial, b_spatial, w_out_ab):
    n_ab = w_in_ab.shape[0]
    kpe_off = Q_RANK + KV_RANK
    kp_w = w_in_ab[:, :, kpe_off:kpe_off + ROPE]
    kp_sw = kp_w[:, :, _SWAP]
    z = lambda c: jnp.zeros((n_ab, D, c), F32)
    wcat = jnp.concatenate(
        [w_in_ab[:, :, :kpe_off], w_in_ab[:, :, kpe_off + ROPE:],
         z(NOPE), kp_w, kp_w, z(NOPE), kp_sw, kp_sw, kp_w, z(HEAD_PAD - ROPE)], axis=2).astype(BF16)
    wq4 = w_q_b.reshape(n_ab, Q_RANK, HEADS, NOPE + ROPE)
    wq = jnp.concatenate([wq4, wq4[..., NOPE:][..., _SWAP]], axis=-1).reshape(n_ab, Q_RANK, QK_W).astype(BF16)
    wkv4 = w_kv_b.reshape(n_ab, KV_RANK, HEADS, NOPE + VDIM)
    wkb = jnp.pad(wkv4[..., :NOPE], ((0, 0), (0, 0), (0, 0), (0, HEAD_PAD - NOPE))).reshape(
        n_ab, KV_RANK, QK_W).astype(BF16)
    wvt = jnp.pad(wkv4[..., NOPE:].transpose(0, 2, 3, 1), ((0, 0), (0, 0), (0, V_ROWS - VDIM), (0, 0))).reshape(
        n_ab, VT_ROWS, KV_RANK).astype(BF16)
    ws = w_spatial.reshape(n_ab, GM_GROUPS // 2, 2, GM_CHUNK, GM_CHUNK).transpose(0, 1, 3, 2, 4).reshape(
        n_ab, GM_GROUPS // 2, GM_CHUNK, 2 * GM_CHUNK).astype(BF16)
    bs = jnp.repeat(b_spatial.transpose(0, 2, 1), GM_W // GM_GROUPS, axis=2)
    aug = np.zeros((VT_ROWS, 1), np.float32)
    aug[VDIM::V_ROWS] = 1.0
    return dict(wcat=wcat, wq=wq, wkb=wkb, wvt=wvt, ws=ws, bs=bs, aug=jnp.asarray(aug),
                qag=q_a_g[:, None], kvag=kv_a_g[:, None], vg=gmlp_v_g[:, None], wo=w_out_ab.astype(BF16))


def kernel(x_prompt, x_sample, cache_ckv, cache_kpe, c, c_ctx, w_mod, b_mod, norm1_g, norm2_g, w_in_ab, q_a_g,
           kv_a_g, w_q_b, w_kv_b, gmlp_v_g, w_spatial, b_spatial, w_out_ab, w_pool, pool_scale, w_ff1, w_ff2,
           final_g):
    batch, seq, _ = x_prompt.shape
    dec_batch, dec_seq, _ = x_sample.shape
    past = cache_ckv.shape[2]
    assert seq == CTX_SEQ_LEN and dec_seq % TM_F == 0 and dec_seq % TM == 0 and past % (2 * LANES) == 0
    lat_tiles = dec_seq // TM_F

    xc = x_prompt.reshape(batch * seq, D)
    xl = x_sample.reshape(dec_batch * dec_seq, D)
    cond8 = jnp.concatenate([c_ctx[None], c, jnp.zeros((MOD_ROWS - 1 - dec_batch, D), F32)], axis=0)
    mod = _modulation(cond8, w_mod, b_mod).reshape(DEPTH, MOD_ROWS, 1, N_MOD * D)
    tabs = _rope_tables(dec_seq)
    bands = _pool_bands()
    place = np.zeros((ROPE, HEAD_PAD), np.float32)
    place[np.arange(ROPE), NOPE + np.arange(ROPE)] = 1.0
    place[np.arange(ROPE), NOPE + ROPE + np.arange(ROPE)] = 1.0
    place = jnp.asarray(place, BF16)

    ev = _prep_even_weights(w_in_ab, q_a_g, kv_a_g, w_q_b, w_kv_b, gmlp_v_g, w_spatial, b_spatial, w_out_ab)
    w1 = w_ff1.astype(BF16)
    w2 = w_ff2.astype(BF16)
    wp = w_pool.astype(BF16)
    n1g = norm1_g[:, None]
    n2g = norm2_g[:, None]
    ps = pool_scale[:, None]

    new_ckv = new_kpe = None
    for l in range(DEPTH):
        final = l == DEPTH - 1
        i = l // 2
        w = dict(n1g=(n1g, l), n2g=(n2g, l), w1=(w1, l), w2=(w2, l), fg=final_g[None], bands=bands)
        if l % 2 == 0:
            w.update({name: (val, i) for name, val in ev.items() if name != "aug"}, aug=ev["aug"])
            qc, kc, vtc, gmc, new_ckv, new_kpe = _in_proj(
                xc, mod[l], w, tabs, layer=i, ckv_prev=new_ckv, kpe_prev=new_kpe)
            ql, kl, vtl, gml = _in_proj(xl, mod[l], w, tabs, layer=i, lat_shape=(dec_batch, dec_seq))
            k_cache, vt_cache = _cache_keys(cache_ckv, cache_kpe, w, place, layer=i)
            attn_c = _attention(qc, [(kc.reshape(batch, seq, QK_W), vtc)], batch=batch, t_len=seq)
            attn_l = _attention(ql, [(k_cache, vt_cache), (kl.reshape(dec_batch, dec_seq, QK_W), vtl)],
                                batch=dec_batch, t_len=dec_seq)
            xc = _ffn_ab(xc, attn_c, gmc, mod[l], w, final, None)
            xl = _ffn_ab(xl, attn_l, gml, mod[l], w, final, lat_tiles)
        else:
            w.update(wp=(wp, i), ps=(ps, i))
            xc = _ffn_pool(xc, mod[l], w, final, None)
            xl = _ffn_pool(xl, mod[l], w, final, lat_tiles)

    return xc.reshape(batch, seq, D), xl.reshape(dec_batch, dec_seq, D), new_ckv, new_kpe
```

```python
import functools
import math

import numpy as np
import jax
import jax.numpy as jnp
from jax import lax
from jax.experimental import pallas as pl
from jax.experimental.pallas import tpu as pltpu

F32 = jnp.float32
BF16 = jnp.bfloat16

D = 1024
DEPTH = 4
N_MOD = 6
HEADS = 8
NOPE = 64
ROPE = 32
VDIM = 64
Q_RANK = 384
KV_RANK = 256
GRID_W = 64
ROPE_BASE = 10000.0
ATTN_SCALE = (NOPE + ROPE) ** -0.5
GM_W = 512
GM_GROUPS = 8
GM_CHUNK = 128
POOL_WINDOWS = (2, 4, 8, 16)
POOL_GD = D // len(POOL_WINDOWS)
D_FF = 4 * D
EPS = 1e-6
CTX_SEQ_LEN = 256

LANES = 128
HEAD_PAD = LANES
QK_W = HEADS * HEAD_PAD
V_ROWS = VDIM + 16
VT_ROWS = HEADS * V_ROWS
MOD_ROWS = 8
HALO = 8
SEG = 256

C_QA = 0
C_KVA = C_QA + Q_RANK
C_U = C_KVA + KV_RANK
C_V = C_U + GM_W
C_KP = C_V + GM_W
C_KPS = C_KP + HEAD_PAD
C_KO = C_KPS + HEAD_PAD
IN_W = C_KO + HEAD_PAD

TM = 1024
TM_F = 1024
TF = 1024
TQ = 256
Q_TILES = 4
MAX_KEY_BLK = 2048
VMEM_LIMIT = 56 * 1024 * 1024


def _rms(x, g):
    return x * lax.rsqrt(jnp.mean(x * x, axis=-1, keepdims=True) + EPS) * g


def _const_spec(shape):
    nd = len(shape)
    return pl.BlockSpec(shape, lambda *_: (0,) * nd)


def _weights(w, names):
    arrs, specs = [], []
    for name in names:
        entry = w[name]
        if isinstance(entry, tuple):
            arr, layer = entry
            shape = arr.shape[1:]
            specs.append(pl.BlockSpec((None,) + shape, lambda *_, layer=layer, nd=len(shape): (layer,) + (0,) * nd,
                                      pipeline_mode=pl.Buffered(1)))
        else:
            arr = entry
            specs.append(pl.BlockSpec(arr.shape, lambda *_, nd=arr.ndim: (0,) * nd, pipeline_mode=pl.Buffered(1)))
        arrs.append(arr)
    return arrs, specs


def _params(n_axes, flags=None):
    return pltpu.CompilerParams(dimension_semantics=("arbitrary",) * n_axes, vmem_limit_bytes=VMEM_LIMIT,
                                flags=flags)


def _mod_kernel(cond_ref, w_ref, b_ref, o_ref):
    c = cond_ref[...]
    s = (c / (1.0 + jnp.exp(-c))).astype(BF16)
    w = w_ref[...].astype(BF16)
    o_ref[...] = jnp.dot(s, w, preferred_element_type=F32) + b_ref[...]


def _modulation(cond8, w_mod, b_mod):
    tn = 1536
    nw = N_MOD * D
    return pl.pallas_call(
        _mod_kernel,
        grid=(DEPTH, nw // tn),
        in_specs=[
            pl.BlockSpec((MOD_ROWS, D), lambda l, j: (0, 0)),
            pl.BlockSpec((None, D, tn), lambda l, j: (l, 0, j)),
            pl.BlockSpec((None, 1, tn), lambda l, j: (l, 0, j)),
        ],
        out_specs=pl.BlockSpec((None, MOD_ROWS, tn), lambda l, j: (l, 0, j)),
        out_shape=jax.ShapeDtypeStruct((DEPTH, MOD_ROWS, nw), F32),
        compiler_params=_params(2),
        name="modulation",
    )(cond8, w_mod, b_mod.reshape(DEPTH, 1, nw))


def _mod_spec(tiles_per_batch):
    if tiles_per_batch is None:
        return pl.BlockSpec((None, 1, N_MOD * D), lambda i: (0, 0, 0))
    return pl.BlockSpec((None, 1, N_MOD * D), lambda i: (1 + i // tiles_per_batch, 0, 0))


def _max_key_norm2(keys):
    kf = keys.astype(F32)
    ksq = kf * kf
    rows = []
    for h in range(HEADS):
        norm2 = jnp.sum(ksq[:, h * HEAD_PAD:(h + 1) * HEAD_PAD], axis=-1, keepdims=True)
        rows.append(jnp.broadcast_to(jnp.max(norm2, axis=0, keepdims=True), (1, LANES)))
    return jnp.concatenate(rows, axis=0)


def _in_kernel(x_ref, mod_ref, n1g_ref, wcat_ref, qag_ref, kvag_ref, wq_ref, wkb_ref, wvt_ref, aug_ref,
               vg_ref, ws_ref, bs_ref, tq_ref, ck_ref, sk_ref, *rest, is_ctx, layer, n_prev):
    prev_refs, out_refs = rest[:n_prev], rest[n_prev:]
    if is_ctx:
        q_ref, k_ref, vt_ref, gm_ref, ckv_ref, kpe_ref = out_refs
    else:
        q_ref, k_ref, vt_ref, gm_ref, kn2_ref = out_refs
    tm = x_ref.shape[0]
    x = x_ref[...]
    sh1 = mod_ref[:, 0 * D:1 * D]
    sc1 = mod_ref[:, 1 * D:2 * D]
    h = (_rms(x, n1g_ref[...]) * (1.0 + sc1) + sh1).astype(BF16)
    z = jnp.dot(h, wcat_ref[...], preferred_element_type=F32)

    qn = _rms(z[:, C_QA:C_QA + Q_RANK], qag_ref[...]).astype(BF16)
    qa = jnp.dot(qn, wq_ref[...], preferred_element_type=F32)
    tq8 = jnp.concatenate([tq_ref[...]] * HEADS, axis=-1)
    q_ref[...] = (qa * tq8 * (ATTN_SCALE * math.log2(math.e))).astype(BF16)

    ckv = _rms(z[:, C_KVA:C_KVA + KV_RANK], kvag_ref[...])
    kp_rot = z[:, C_KP:C_KP + HEAD_PAD] * ck_ref[...] + z[:, C_KPS:C_KPS + HEAD_PAD] * sk_ref[...]
    ckv_b = ckv.astype(BF16)
    kn = jnp.dot(ckv_b, wkb_ref[...], preferred_element_type=F32)
    keys = (kn + jnp.concatenate([kp_rot] * HEADS, axis=-1)).astype(BF16)
    k_ref[...] = keys
    if not is_ctx:
        kn2_ref[...] = _max_key_norm2(keys)
    vt = lax.dot_general(wvt_ref[...], ckv_b, (((1,), (1,)), ((), ())), preferred_element_type=F32)
    vt_ref[...] = (vt + aug_ref[...]).astype(BF16)
    if is_ctx:
        seqs, n_ab = ckv_ref.shape[0], ckv_ref.shape[1]
        for slot in range(n_ab):
            if slot == layer:
                ckv_ref[:, slot] = ckv.reshape(seqs, CTX_SEQ_LEN, KV_RANK)
                kpe_ref[:, slot] = z[:, C_KO:C_KO + ROPE].reshape(seqs, CTX_SEQ_LEN, ROPE)
            elif prev_refs:
                ckv_ref[:, slot] = prev_refs[0][:, slot]
                kpe_ref[:, slot] = prev_refs[1][:, slot]
            else:
                ckv_ref[:, slot] = jnp.zeros((seqs, CTX_SEQ_LEN, KV_RANK), F32)
                kpe_ref[:, slot] = jnp.zeros((seqs, CTX_SEQ_LEN, ROPE), F32)

    u = jax.nn.gelu(z[:, C_U:C_U + GM_W])
    vv = jax.nn.gelu(z[:, C_V:C_V + GM_W])
    vn = _rms(vv, vg_ref[...])
    lane = lax.broadcasted_iota(jnp.int32, (GM_CHUNK, LANES), 1)
    lo = lane < GM_W // GM_GROUPS
    for c in range(tm // GM_CHUNK):
        rows = slice(c * GM_CHUNK, (c + 1) * GM_CHUNK)
        for j in range(GM_W // LANES):
            cols = slice(j * LANES, (j + 1) * LANES)
            blk = vn[rows, cols]
            rhs = jnp.concatenate([jnp.where(lo, blk, 0.0), jnp.where(lo, 0.0, blk)], axis=0).astype(BF16)
            s = jnp.dot(ws_ref[j], rhs, preferred_element_type=F32) + bs_ref[:, cols]
            gm_ref[rows, cols] = (u[rows, cols] * s).astype(BF16)


def _in_proj(x, mod_l, w, tabs, *, layer, ckv_prev=None, kpe_prev=None, lat_shape=None):
    n = x.shape[0]
    is_ctx = lat_shape is None
    row = lambda i: (i, 0)
    id_blk = tabs[0].shape[0] // TM - 1
    if is_ctx:
        tpb = None
        tab_idx = lambda i: (id_blk, 0)
        vt_shape, vt_spec = (VT_ROWS, n), pl.BlockSpec((VT_ROWS, TM), lambda i: (0, i))
    else:
        dec_batch, dec_seq = lat_shape
        tpb = dec_seq // TM
        tab_idx = lambda i: (i % tpb, 0)
        vt_shape = (dec_batch, VT_ROWS, dec_seq)
        vt_spec = pl.BlockSpec((None, VT_ROWS, TM), lambda i: (i // tpb, 0, i % tpb))
    w_arrs, w_specs = _weights(w, ["n1g", "wcat", "qag", "kvag", "wq", "wkb", "wvt", "aug", "vg", "ws", "bs"])
    in_specs = [pl.BlockSpec((TM, D), row), _mod_spec(tpb)] + w_specs + [pl.BlockSpec((TM, HEAD_PAD), tab_idx)] * 3
    args = [x, mod_l, *w_arrs, *tabs]
    out_shape = [jax.ShapeDtypeStruct((n, QK_W), BF16), jax.ShapeDtypeStruct((n, QK_W), BF16),
                 jax.ShapeDtypeStruct(vt_shape, BF16), jax.ShapeDtypeStruct((n, GM_W), BF16)]
    out_specs = [pl.BlockSpec((TM, QK_W), row), pl.BlockSpec((TM, QK_W), row), vt_spec,
                 pl.BlockSpec((TM, GM_W), row)]
    aliases = {}
    n_prev = 0
    if not is_ctx:
        out_shape.append(jax.ShapeDtypeStruct((n // TM, HEADS, LANES), F32))
        out_specs.append(pl.BlockSpec((None, HEADS, LANES), lambda i: (i, 0, 0)))
    if is_ctx:
        seqs = TM // CTX_SEQ_LEN
        batch = n // CTX_SEQ_LEN
        n_ab = DEPTH // 2
        cache_specs = [pl.BlockSpec((seqs, n_ab, CTX_SEQ_LEN, KV_RANK), lambda i: (i, 0, 0, 0)),
                       pl.BlockSpec((seqs, n_ab, CTX_SEQ_LEN, ROPE), lambda i: (i, 0, 0, 0))]
        out_shape += [jax.ShapeDtypeStruct((batch, n_ab, CTX_SEQ_LEN, KV_RANK), F32),
                      jax.ShapeDtypeStruct((batch, n_ab, CTX_SEQ_LEN, ROPE), F32)]
        out_specs += cache_specs
        if ckv_prev is not None:
            in_specs += cache_specs
            args += [ckv_prev, kpe_prev]
            aliases = {len(args) - 2: 4, len(args) - 1: 5}
            n_prev = 2

    return pl.pallas_call(
        functools.partial(_in_kernel, is_ctx=is_ctx, layer=layer, n_prev=n_prev),
        grid=(n // TM,),
        in_specs=in_specs,
        out_specs=out_specs,
        out_shape=out_shape,
        input_output_aliases=aliases,
        compiler_params=_params(1),
        name="in_proj_ctx" if is_ctx else "in_proj_lat",
    )(*args)


def _cache_kernel(ckv_ref, kpe_ref, wkb_ref, wvt_ref, aug_ref, place_ref, k_ref, vt_ref, kn2_ref):
    ckv_b = ckv_ref[...].astype(BF16)
    kn = jnp.dot(ckv_b, wkb_ref[...], preferred_element_type=F32)
    kp = jnp.dot(kpe_ref[...].astype(BF16), place_ref[...], preferred_element_type=F32)
    keys = (kn + jnp.concatenate([kp] * HEADS, axis=-1)).astype(BF16)
    k_ref[...] = keys
    kn2_ref[...] = _max_key_norm2(keys)
    vt = lax.dot_general(wvt_ref[...], ckv_b, (((1,), (1,)), ((), ())), preferred_element_type=F32)
    vt_ref[...] = (vt + aug_ref[...]).astype(BF16)


def _cache_keys(cache_ckv, cache_kpe, w, place, *, layer):
    dec_batch, _, past, _ = cache_ckv.shape
    w_arrs, w_specs = _weights(dict(w, place=place), ["wkb", "wvt", "aug", "place"])
    return pl.pallas_call(
        _cache_kernel,
        grid=(dec_batch,),
        in_specs=[
            pl.BlockSpec((None, None, past, KV_RANK), lambda b: (b, layer, 0, 0)),
            pl.BlockSpec((None, None, past, ROPE), lambda b: (b, layer, 0, 0)),
            *w_specs,
        ],
        out_specs=[pl.BlockSpec((None, past, QK_W), lambda b: (b, 0, 0)),
                   pl.BlockSpec((None, VT_ROWS, past), lambda b: (b, 0, 0)),
                   pl.BlockSpec((None, HEADS, LANES), lambda b: (b, 0, 0))],
        out_shape=(jax.ShapeDtypeStruct((dec_batch, past, QK_W), BF16),
                   jax.ShapeDtypeStruct((dec_batch, VT_ROWS, past), BF16),
                   jax.ShapeDtypeStruct((dec_batch, HEADS, LANES), F32)),
        compiler_params=_params(1),
        name="cache_keys",
    )(cache_ckv, cache_kpe, *w_arrs)


SAFE_DENOM = 2.0 ** -100
BOUND_SLACK = 1.02


def _attn_bounded(q_ref, srcs, kmax_ref, o_ref, st_ref, p_ref):
    blocks = []
    base = 0
    for k_ref, vt_ref in srcs:
        l_src = k_ref.shape[0]
        for start in range(0, l_src, MAX_KEY_BLK):
            size = min(MAX_KEY_BLK, l_src - start)
            blocks.append((k_ref, vt_ref, slice(start, start + size), slice(base + start, base + start + size)))
        base += l_src
    n_qt = q_ref.shape[0] // TQ
    items = [(qt, h) for qt in range(n_qt) for h in range(HEADS)]
    n_items = len(items)
    ones = jnp.ones((8, HEAD_PAD), BF16)

    def finish(ot, qt, h, pair):
        pair[h] = ot[0:VDIM] * (1.0 / ot[VDIM:VDIM + 1])
        if h % 2 == 1:
            both = jnp.concatenate([pair.pop(h - 1), pair.pop(h)], axis=0)
            o_ref[qt * TQ:(qt + 1) * TQ, (h // 2) * LANES:(h // 2 + 1) * LANES] = both.T.astype(BF16)

    pair = {}
    lmin = None
    for step in range(n_items + 1):
        ia, ic = step, step - 1
        ot = None
        if ia < n_items:
            qt, h = items[ia]
            cols = slice(h * HEAD_PAD, (h + 1) * HEAD_PAD)
            qh = q_ref[qt * TQ:(qt + 1) * TQ, cols]
            qf = qh.astype(F32)
            qn2 = lax.dot_general(ones, (qf * qf).astype(BF16), (((1,), (1,)), ((), ())),
                                  preferred_element_type=F32)[0:1]
            km2 = kmax_ref[h:h + 1, :]
            shift = jnp.sqrt(qn2 * jnp.concatenate([km2] * (TQ // LANES), axis=-1)) * BOUND_SLACK
        for k_ref, vt_ref, src_rows, rows in blocks:
            if ia < n_items:
                st = lax.dot_general(k_ref[src_rows, cols], qh, (((1,), (1,)), ((), ())),
                                     preferred_element_type=F32)
                p_ref[ia % 2, rows, :] = jnp.exp2(st - shift).astype(BF16)
            if 0 <= ic < n_items:
                hc = items[ic][1]
                part = jnp.dot(vt_ref[hc * V_ROWS:(hc + 1) * V_ROWS, src_rows], p_ref[ic % 2, rows, :],
                               preferred_element_type=F32)
                ot = part if ot is None else ot + part
        if 0 <= ic < n_items:
            denom = ot[VDIM:VDIM + 1]
            lmin = denom if lmin is None else jnp.minimum(lmin, denom)
            finish(ot, *items[ic], pair)

    @pl.when(jnp.logical_not(jnp.min(lmin) >= SAFE_DENOM))
    def _():
        def redo(i, carry):
            qt = i // (HEADS // 2)
            hp = i % (HEADS // 2)
            q_rows = pl.ds(pl.multiple_of(qt * TQ, TQ), TQ)
            outs = []
            for e in range(2):
                h = 2 * hp + e
                cols = pl.ds(pl.multiple_of(h * HEAD_PAD, HEAD_PAD), HEAD_PAD)
                qh = q_ref[q_rows, cols]
                for k_ref, _, src_rows, rows in blocks:
                    st_ref[0, rows, :] = lax.dot_general(k_ref[src_rows, cols], qh, (((1,), (1,)), ((), ())),
                                                         preferred_element_type=F32)
                m = jnp.max(st_ref[0], axis=0, keepdims=True)
                p_ref[0] = jnp.exp2(st_ref[0] - m).astype(BF16)
                ot = None
                v_rows = pl.ds(pl.multiple_of(h * V_ROWS, 16), V_ROWS)
                for _, vt_ref, src_rows, rows in blocks:
                    part = jnp.dot(vt_ref[v_rows, src_rows], p_ref[0, rows, :], preferred_element_type=F32)
                    ot = part if ot is None else ot + part
                outs.append(ot[0:VDIM] * (1.0 / ot[VDIM:VDIM + 1]))
            o_ref[q_rows, pl.ds(pl.multiple_of(hp * LANES, LANES), LANES)] = (
                jnp.concatenate(outs, axis=0).T.astype(BF16))
            return carry

        lax.fori_loop(0, n_qt * (HEADS // 2), redo, 0)


def _attn_kernel(q_ref, *refs, n_src, segment, kmax_bound):
    srcs = [(refs[2 * s], refs[2 * s + 1]) for s in range(n_src)]
    if kmax_bound:
        kmax_ref, o_ref, st_ref, p_ref = refs[2 * n_src:]
        _attn_bounded(q_ref, srcs, kmax_ref, o_ref, st_ref, p_ref)
        return
    o_ref, st_ref, p_ref = refs[2 * n_src:]

    def key_blocks(qt):
        blocks = []
        base = 0
        for k_ref, vt_ref in srcs:
            first, l_src = (0, k_ref.shape[0]) if segment is None else (qt * segment, segment)
            for start in range(0, l_src, MAX_KEY_BLK):
                size = min(MAX_KEY_BLK, l_src - start)
                blocks.append((k_ref, vt_ref, slice(first + start, first + start + size),
                               slice(base + start, base + start + size)))
            base += l_src
        return blocks

    n_st = st_ref.shape[0]
    items = [(qt, h) for qt in range(q_ref.shape[0] // TQ) for h in range(HEADS)]
    n_items = len(items)
    m = [None] * n_items
    pair = {}
    for step in range(n_items + 2):
        ia, ib, ic = step, step - 1, step - 2
        mrun = None
        ot = None
        for j in range(len(key_blocks(0))):
            if ia < n_items:
                qt, h = items[ia]
                k_ref, _, src_rows, rows = key_blocks(qt)[j]
                cols = slice(h * HEAD_PAD, (h + 1) * HEAD_PAD)
                st = lax.dot_general(k_ref[src_rows, cols], q_ref[qt * TQ:(qt + 1) * TQ, cols],
                                     (((1,), (1,)), ((), ())), preferred_element_type=F32)
                st_ref[ia % n_st, rows, :] = st
                blk_max = jnp.max(st.reshape(st.shape[0] // 8, 8, TQ), axis=0)
                mrun = blk_max if mrun is None else jnp.maximum(mrun, blk_max)
            if 0 <= ib < n_items:
                rows = key_blocks(0)[j][3]
                p_ref[ib % 2, rows, :] = jnp.exp2(st_ref[ib % n_st, rows, :] - m[ib]).astype(BF16)
            if 0 <= ic < n_items:
                qt, h = items[ic]
                _, vt_ref, src_rows, rows = key_blocks(qt)[j]
                part = jnp.dot(vt_ref[h * V_ROWS:(h + 1) * V_ROWS, src_rows], p_ref[ic % 2, rows, :],
                               preferred_element_type=F32)
                ot = part if ot is None else ot + part
        if ia < n_items:
            m[ia] = jnp.max(mrun, axis=0, keepdims=True)
        if 0 <= ic < n_items:
            qt, h = items[ic]
            pair[h] = ot[0:VDIM] * (1.0 / ot[VDIM:VDIM + 1])
            if h % 2 == 1:
                both = jnp.concatenate([pair.pop(h - 1), pair.pop(h)], axis=0)
                o_ref[qt * TQ:(qt + 1) * TQ, (h // 2) * LANES:(h // 2 + 1) * LANES] = both.T.astype(BF16)


def _attention(q, sources, *, batch, t_len, segment=None, kmax2=None):
    tq = min(t_len, Q_TILES * TQ)
    nq = t_len // tq
    in_specs = [pl.BlockSpec((tq, QK_W), lambda b, qi: (b * nq + qi, 0))]
    args = [q]
    l_total = 0
    for k, vt in sources:
        if segment is None:
            l_len = k.shape[1]
            l_total += l_len
            in_specs += [pl.BlockSpec((None, l_len, QK_W), lambda b, qi: (b, 0, 0)),
                         pl.BlockSpec((None, VT_ROWS, l_len), lambda b, qi: (b, 0, 0))]
        else:
            assert segment == TQ and nq == 1
            l_total += segment
            in_specs += [pl.BlockSpec((tq, QK_W), lambda b, qi: (b, 0)),
                         pl.BlockSpec((VT_ROWS, tq), lambda b, qi: (0, b))]
        args += [k, vt]
    if kmax2 is not None:
        in_specs.append(pl.BlockSpec((None, HEADS, LANES), lambda b, qi: (b, 0, 0)))
        args.append(kmax2)
    return pl.pallas_call(
        functools.partial(_attn_kernel, n_src=len(sources), segment=segment, kmax_bound=kmax2 is not None),
        grid=(batch, nq),
        scratch_shapes=[pltpu.VMEM((1 if kmax2 is not None else 2, l_total, TQ), F32),
                        pltpu.VMEM((2, l_total, TQ), BF16)],
        in_specs=in_specs,
        out_specs=pl.BlockSpec((tq, HEADS * VDIM), lambda b, qi: (b * nq + qi, 0)),
        out_shape=jax.ShapeDtypeStruct((batch * t_len, HEADS * VDIM), BF16),
        compiler_params=_params(2),
        name="attention",
    )(*args)


def _mlp_tail(x1, mod_ref, n2g_ref, w1_ref, w2_ref, fg_ref, o_ref, final):
    sh2 = mod_ref[:, 3 * D:4 * D]
    sc2 = mod_ref[:, 4 * D:5 * D]
    g2 = mod_ref[:, 5 * D:6 * D]
    h = (_rms(x1, n2g_ref[...]) * (1.0 + sc2) + sh2).astype(BF16)
    acc = None
    for c in range(D_FF // TF):
        a = jnp.maximum(jnp.dot(h, w1_ref[:, c * TF:(c + 1) * TF], preferred_element_type=F32), 0.0)
        part = jnp.dot((a * a).astype(BF16), w2_ref[c * TF:(c + 1) * TF, :], preferred_element_type=F32)
        acc = part if acc is None else acc + part
    x2 = x1 + g2 * acc
    if final:
        x2 = _rms(x2, fg_ref[...])
    o_ref[...] = x2


def _ffn_ab_kernel(x_ref, attn_ref, gm_ref, mod_ref, wo_ref, n2g_ref, w1_ref, w2_ref, fg_ref, o_ref, *, final):
    g1 = mod_ref[:, 2 * D:3 * D]
    mix = (jnp.dot(attn_ref[...], wo_ref[0:HEADS * VDIM, :], preferred_element_type=F32)
           + jnp.dot(gm_ref[...], wo_ref[HEADS * VDIM:, :], preferred_element_type=F32))
    x1 = x_ref[...] + g1 * mix
    _mlp_tail(x1, mod_ref, n2g_ref, w1_ref, w2_ref, fg_ref, o_ref, final)


def _ffn_pool_kernel(x_ref, xp_ref, xn_ref, mod_ref, n1g_ref, band_ref, wp_ref, ps_ref, n2g_ref, w1_ref, w2_ref,
                     fg_ref, o_ref, *, final, tiles_per_seq):
    sh1 = mod_ref[:, 0 * D:1 * D]
    sc1 = mod_ref[:, 1 * D:2 * D]
    g1 = mod_ref[:, 2 * D:3 * D]
    n1g = n1g_ref[...]
    ada = lambda v: _rms(v, n1g) * (1.0 + sc1) + sh1
    x = x_ref[...]
    ht = ada(x)
    zero_halo = jnp.zeros((HALO, D), F32)
    if tiles_per_seq is None:
        seq_len, pos0 = SEG, 0
    else:
        t = pl.program_id(0) % tiles_per_seq
        seq_len, pos0 = tiles_per_seq * TM_F, t * TM_F
        tile_prev = jnp.where(t != 0, ada(xp_ref[...]), 0.0)
        tile_next = jnp.where(t != tiles_per_seq - 1, ada(xn_ref[...]), 0.0)
    n_seg = TM_F // SEG
    seg_rows = []
    for s in range(n_seg):
        seg = ht[s * SEG:(s + 1) * SEG]
        if tiles_per_seq is None:
            prev, nxt, base = zero_halo, zero_halo, 0
        else:
            prev = tile_prev if s == 0 else ht[s * SEG - HALO:s * SEG]
            nxt = tile_next if s == n_seg - 1 else ht[(s + 1) * SEG:(s + 1) * SEG + HALO]
            base = pos0 + s * SEG
        hs = jnp.concatenate([prev, seg, nxt], axis=0)
        hi = hs.astype(BF16)
        lo = (hs - hi.astype(F32)).astype(BF16)
        pos = base + lax.broadcasted_iota(jnp.int32, (SEG, POOL_GD), 0)
        parts = []
        for gi, w in enumerate(POOL_WINDOWS):
            half = w // 2
            cols = slice(gi * POOL_GD, (gi + 1) * POOL_GD)
            top = (jnp.dot(band_ref[gi, 0], hi[0:SEG, cols], preferred_element_type=F32)
                   + jnp.dot(band_ref[gi, 0], lo[0:SEG, cols], preferred_element_type=F32))
            bot = (jnp.dot(band_ref[gi, 1], hi[2 * HALO:, cols], preferred_element_type=F32)
                   + jnp.dot(band_ref[gi, 1], lo[2 * HALO:, cols], preferred_element_type=F32))
            tot = jnp.concatenate([top, bot], axis=0)
            cnt = (jnp.minimum(pos + half, seq_len) - jnp.maximum(pos - half, 0)).astype(F32)
            diff = (tot / cnt - seg[:, cols]).astype(BF16)
            parts.append(jnp.dot(diff, wp_ref[gi], preferred_element_type=F32))
        seg_rows.append(jnp.concatenate(parts, axis=-1))
    mix = jnp.concatenate(seg_rows, axis=0) * ps_ref[...]
    x1 = x + g1 * mix
    _mlp_tail(x1, mod_ref, n2g_ref, w1_ref, w2_ref, fg_ref, o_ref, final)


def _pool_bands():
    half_rows = SEG // 2
    t = np.arange(half_rows)[:, None]
    u = np.arange(SEG)[None, :]
    bands = np.zeros((len(POOL_WINDOWS), 2, half_rows, SEG), np.float32)
    for gi, w in enumerate(POOL_WINDOWS):
        half = w // 2
        d_top = (u - HALO) - t
        d_bot = (u + HALO) - (half_rows + t)
        bands[gi, 0] = (d_top >= -half) & (d_top <= half - 1)
        bands[gi, 1] = (d_bot >= -half) & (d_bot <= half - 1)
    return jnp.asarray(bands, BF16)


def _ffn_ab(x, attn, gm, mod_l, w, final, tiles_per_batch):
    n = x.shape[0]
    row = lambda i: (i, 0)
    w_arrs, w_specs = _weights(w, ["wo", "n2g", "w1", "w2", "fg"])
    return pl.pallas_call(
        functools.partial(_ffn_ab_kernel, final=final),
        grid=(n // TM_F,),
        in_specs=[
            pl.BlockSpec((TM_F, D), row),
            pl.BlockSpec((TM_F, HEADS * VDIM), row),
            pl.BlockSpec((TM_F, GM_W), row),
            _mod_spec(tiles_per_batch),
        ] + w_specs,
        out_specs=pl.BlockSpec((TM_F, D), row),
        out_shape=jax.ShapeDtypeStruct((n, D), F32),
        compiler_params=_params(1),
        name="ffn_ab",
    )(x, attn, gm, mod_l, *w_arrs)


def _ffn_pool(x, mod_l, w, final, tiles_per_batch):
    n = x.shape[0]
    row = lambda i: (i, 0)
    hb = TM_F // HALO
    w_arrs, w_specs = _weights(w, ["n1g", "bands", "wp", "ps", "n2g", "w1", "w2", "fg"])
    return pl.pallas_call(
        functools.partial(_ffn_pool_kernel, final=final, tiles_per_seq=tiles_per_batch),
        grid=(n // TM_F,),
        in_specs=[
            pl.BlockSpec((TM_F, D), row),
            pl.BlockSpec((HALO, D), lambda i: (jnp.maximum(i * hb - 1, 0), 0)),
            pl.BlockSpec((HALO, D), lambda i: (jnp.minimum((i + 1) * hb, n // HALO - 1), 0)),
            _mod_spec(tiles_per_batch),
        ] + w_specs,
        out_specs=pl.BlockSpec((TM_F, D), row),
        out_shape=jax.ShapeDtypeStruct((n, D), F32),
        compiler_params=_params(1),
        name="ffn_pool",
    )(x, x, x, mod_l, *w_arrs)


_SWAP = np.arange(ROPE).reshape(2, 2, ROPE // 4)[:, ::-1, :].reshape(ROPE)


def _rope_tables(t_len):
    rows = t_len // GRID_W
    row = jnp.repeat(jnp.arange(rows), GRID_W).astype(F32)
    col = jnp.tile(jnp.arange(GRID_W), rows).astype(F32)
    axis_dim = ROPE // 2
    inv = ROPE_BASE ** (-jnp.arange(0, axis_dim, 2, dtype=F32) / axis_dim)
    ang = jnp.stack([row[:, None] * inv, col[:, None] * inv], axis=1)
    cos = jnp.cos(ang)
    sin = jnp.sin(ang)
    c32 = jnp.stack([cos, cos], axis=2).reshape(t_len, ROPE)
    s32 = jnp.stack([-sin, sin], axis=2).reshape(t_len, ROPE)
    one = lambda r, c: jnp.ones((r, c), F32)
    zero = lambda r, c: jnp.zeros((r, c), F32)
    tq = jnp.concatenate([jnp.concatenate([one(t_len, NOPE), c32, s32], -1),
                          jnp.concatenate([one(TM, NOPE + ROPE), zero(TM, ROPE)], -1)], 0)
    ck = jnp.concatenate([jnp.concatenate([zero(t_len, NOPE), c32, c32], -1),
                          jnp.concatenate([zero(TM, NOPE), one(TM, 2 * ROPE)], -1)], 0)
    sk = jnp.concatenate([jnp.concatenate([zero(t_len, NOPE), s32, s32], -1), zero(TM, HEAD_PAD)], 0)
    return tq, ck, sk


def _prep_even_weights(w_in_ab, q_a_g, kv_a_g, w_q_b, w_kv_b, gmlp_v_g, w_spatial, b_spatial, w_out_ab):
    n_ab = w_in_ab.shape[0]
    kpe_off = Q_RANK + KV_RANK
    kp_w = w_in_ab[:, :, kpe_off:kpe_off + ROPE]
    kp_sw = kp_w[:, :, _SWAP]
    z = lambda c: jnp.zeros((n_ab, D, c), F32)
    wcat = jnp.concatenate(
        [w_in_ab[:, :, :kpe_off], w_in_ab[:, :, kpe_off + ROPE:],
         z(NOPE), kp_w, kp_w, z(NOPE), kp_sw, kp_sw, kp_w, z(HEAD_PAD - ROPE)], axis=2).astype(BF16)
    wq4 = w_q_b.reshape(n_ab, Q_RANK, HEADS, NOPE + ROPE)
    wq = jnp.concatenate([wq4, wq4[..., NOPE:][..., _SWAP]], axis=-1).reshape(n_ab, Q_RANK, QK_W).astype(BF16)
    wkv4 = w_kv_b.reshape(n_ab, KV_RANK, HEADS, NOPE + VDIM)
    wkb = jnp.pad(wkv4[..., :NOPE], ((0, 0), (0, 0), (0, 0), (0, HEAD_PAD - NOPE))).reshape(
        n_ab, KV_RANK, QK_W).astype(BF16)
    wvt = jnp.pad(wkv4[..., NOPE:].transpose(0, 2, 3, 1), ((0, 0), (0, 0), (0, V_ROWS - VDIM), (0, 0))).reshape(
        n_ab, VT_ROWS, KV_RANK).astype(BF16)
    ws = w_spatial.reshape(n_ab, GM_GROUPS // 2, 2, GM_CHUNK, GM_CHUNK).transpose(0, 1, 3, 2, 4).reshape(
        n_ab, GM_GROUPS // 2, GM_CHUNK, 2 * GM_CHUNK).astype(BF16)
    bs = jnp.repeat(b_spatial.transpose(0, 2, 1), GM_W // GM_GROUPS, axis=2)
    aug = np.zeros((VT_ROWS, 1), np.float32)
    aug[VDIM::V_ROWS] = 1.0
    return dict(wcat=wcat, wq=wq, wkb=wkb, wvt=wvt, ws=ws, bs=bs, aug=jnp.asarray(aug),
                qag=q_a_g[:, None], kvag=kv_a_g[:, None], vg=gmlp_v_g[:, None], wo=w_out_ab.astype(BF16))


def kernel(x_prompt, x_sample, cache_ckv, cache_kpe, c, c_ctx, w_mod, b_mod, norm1_g, norm2_g, w_in_ab, q_a_g,
           kv_a_g, w_q_b, w_kv_b, gmlp_v_g, w_spatial, b_spatial, w_out_ab, w_pool, pool_scale, w_ff1, w_ff2,
           final_g):
    batch, seq, _ = x_prompt.shape
    dec_batch, dec_seq, _ = x_sample.shape
    past = cache_ckv.shape[2]
    assert seq == CTX_SEQ_LEN and dec_seq % TM_F == 0 and dec_seq % TM == 0 and past % (2 * LANES) == 0
    lat_tiles = dec_seq // TM_F

    xc = x_prompt.reshape(batch * seq, D)
    xl = x_sample.reshape(dec_batch * dec_seq, D)
    cond8 = jnp.concatenate([c_ctx[None], c, jnp.zeros((MOD_ROWS - 1 - dec_batch, D), F32)], axis=0)
    mod = _modulation(cond8, w_mod, b_mod).reshape(DEPTH, MOD_ROWS, 1, N_MOD * D)
    tabs = _rope_tables(dec_seq)
    bands = _pool_bands()
    place = np.zeros((ROPE, HEAD_PAD), np.float32)
    place[np.arange(ROPE), NOPE + np.arange(ROPE)] = 1.0
    place[np.arange(ROPE), NOPE + ROPE + np.arange(ROPE)] = 1.0
    place = jnp.asarray(place, BF16)

    ev = _prep_even_weights(w_in_ab, q_a_g, kv_a_g, w_q_b, w_kv_b, gmlp_v_g, w_spatial, b_spatial, w_out_ab)
    w1 = w_ff1.astype(BF16)
    w2 = w_ff2.astype(BF16)
    wp = w_pool.astype(BF16)
    n1g = norm1_g[:, None]
    n2g = norm2_g[:, None]
    ps = pool_scale[:, None]

    new_ckv = new_kpe = None
    for l in range(DEPTH):
        final = l == DEPTH - 1
        i = l // 2
        w = dict(n1g=(n1g, l), n2g=(n2g, l), w1=(w1, l), w2=(w2, l), fg=final_g[None], bands=bands)
        if l % 2 == 0:
            w.update({name: (val, i) for name, val in ev.items() if name != "aug"}, aug=ev["aug"])
            qc, kc, vtc, gmc, new_ckv, new_kpe = _in_proj(
                xc, mod[l], w, tabs, layer=i, ckv_prev=new_ckv, kpe_prev=new_kpe)
            ql, kl, vtl, gml, kn2_l = _in_proj(xl, mod[l], w, tabs, layer=i, lat_shape=(dec_batch, dec_seq))
            k_cache, vt_cache, kn2_c = _cache_keys(cache_ckv, cache_kpe, w, place, layer=i)
            kmax2 = jnp.maximum(kn2_l.reshape(dec_batch, -1, HEADS, LANES).max(axis=1), kn2_c)
            attn_c = _attention(qc, [(kc, vtc)], batch=batch // Q_TILES, t_len=Q_TILES * seq, segment=seq)
            attn_l = _attention(ql, [(k_cache, vt_cache), (kl.reshape(dec_batch, dec_seq, QK_W), vtl)],
                                batch=dec_batch, t_len=dec_seq, kmax2=kmax2)
            xc = _ffn_ab(xc, attn_c, gmc, mod[l], w, final, None)
            xl = _ffn_ab(xl, attn_l, gml, mod[l], w, final, lat_tiles)
        else:
            w.update(wp=(wp, i), ps=(ps, i))
            xc = _ffn_pool(xc, mod[l], w, final, None)
            xl = _ffn_pool(xl, mod[l], w, final, lat_tiles)

    return xc.reshape(batch, seq, D), xl.reshape(dec_batch, dec_seq, D), new_ckv, new_kpe
```

```python
import functools
import math

import numpy as np
import jax
import jax.numpy as jnp
from jax import lax
from jax.experimental import pallas as pl
from jax.experimental.pallas import tpu as pltpu

F32 = jnp.float32
BF16 = jnp.bfloat16

D = 1024
DEPTH = 4
N_MOD = 6
HEADS = 8
NOPE = 64
ROPE = 32
VDIM = 64
Q_RANK = 384
KV_RANK = 256
GRID_W = 64
ROPE_BASE = 10000.0
ATTN_SCALE = (NOPE + ROPE) ** -0.5
GM_W = 512
GM_GROUPS = 8
GM_CHUNK = 128
POOL_WINDOWS = (2, 4, 8, 16)
POOL_GD = D // len(POOL_WINDOWS)
D_FF = 4 * D
EPS = 1e-6
CTX_SEQ_LEN = 256

LANES = 128
HEAD_PAD = LANES
QK_W = HEADS * HEAD_PAD
V_ROWS = VDIM + 16
VT_ROWS = HEADS * V_ROWS
MOD_ROWS = 8
HALO = 8
SEG = 256

C_QA = 0
C_KVA = C_QA + Q_RANK
C_U = C_KVA + KV_RANK
C_V = C_U + GM_W
C_KP = C_V + GM_W
C_KPS = C_KP + HEAD_PAD
C_KO = C_KPS + HEAD_PAD
IN_W = C_KO + HEAD_PAD

TM = 1024
TM_F = 1024
TF = 1024
TQ = 256
Q_TILES = 4
MAX_KEY_BLK = 2048
VMEM_LIMIT = 56 * 1024 * 1024


def _rms(x, g):
    return x * lax.rsqrt(jnp.mean(x * x, axis=-1, keepdims=True) + EPS) * g


def _const_spec(shape):
    nd = len(shape)
    return pl.BlockSpec(shape, lambda *_: (0,) * nd)


def _weights(w, names):
    arrs, specs = [], []
    for name in names:
        entry = w[name]
        if isinstance(entry, tuple):
            arr, layer = entry
            shape = arr.shape[1:]
            specs.append(pl.BlockSpec((None,) + shape, lambda *_, layer=layer, nd=len(shape): (layer,) + (0,) * nd,
                                      pipeline_mode=pl.Buffered(1)))
        else:
            arr = entry
            specs.append(pl.BlockSpec(arr.shape, lambda *_, nd=arr.ndim: (0,) * nd, pipeline_mode=pl.Buffered(1)))
        arrs.append(arr)
    return arrs, specs


def _params(n_axes, flags=None):
    return pltpu.CompilerParams(dimension_semantics=("arbitrary",) * n_axes, vmem_limit_bytes=VMEM_LIMIT,
                                flags=flags)


def _mod_kernel(cond_ref, w_ref, b_ref, o_ref):
    c = cond_ref[...]
    s = (c / (1.0 + jnp.exp(-c))).astype(BF16)
    w = w_ref[...].astype(BF16)
    o_ref[...] = jnp.dot(s, w, preferred_element_type=F32) + b_ref[...]


def _modulation(cond8, w_mod, b_mod):
    tn = 1536
    nw = N_MOD * D
    return pl.pallas_call(
        _mod_kernel,
        grid=(DEPTH, nw // tn),
        in_specs=[
            pl.BlockSpec((MOD_ROWS, D), lambda l, j: (0, 0)),
            pl.BlockSpec((None, D, tn), lambda l, j: (l, 0, j)),
            pl.BlockSpec((None, 1, tn), lambda l, j: (l, 0, j)),
        ],
        out_specs=pl.BlockSpec((None, MOD_ROWS, tn), lambda l, j: (l, 0, j)),
        out_shape=jax.ShapeDtypeStruct((DEPTH, MOD_ROWS, nw), F32),
        compiler_params=_params(2),
        name="modulation",
    )(cond8, w_mod, b_mod.reshape(DEPTH, 1, nw))


def _mod_spec(tiles_per_batch):
    if tiles_per_batch is None:
        return pl.BlockSpec((None, 1, N_MOD * D), lambda i: (0, 0, 0))
    return pl.BlockSpec((None, 1, N_MOD * D), lambda i: (1 + i // tiles_per_batch, 0, 0))


def _max_key_norm2(keys):
    kf = keys.astype(F32)
    ksq = kf * kf
    rows = []
    for h in range(HEADS):
        norm2 = jnp.sum(ksq[:, h * HEAD_PAD:(h + 1) * HEAD_PAD], axis=-1, keepdims=True)
        rows.append(jnp.broadcast_to(jnp.max(norm2, axis=0, keepdims=True), (1, LANES)))
    return jnp.concatenate(rows, axis=0)


def _in_kernel(x_ref, mod_ref, n1g_ref, wcat_ref, qag_ref, kvag_ref, wq_ref, wkb_ref, wvt_ref, aug_ref,
               vg_ref, ws_ref, bs_ref, tq_ref, ck_ref, sk_ref, *rest, is_ctx, layer, n_prev):
    prev_refs, out_refs = rest[:n_prev], rest[n_prev:]
    q_ref, k_ref, vt_ref, gm_ref, kn2_ref = out_refs[:5]
    if is_ctx:
        ckv_ref, kpe_ref = out_refs[5:]
    tm = x_ref.shape[0]
    x = x_ref[...]
    sh1 = mod_ref[:, 0 * D:1 * D]
    sc1 = mod_ref[:, 1 * D:2 * D]
    h = (_rms(x, n1g_ref[...]) * (1.0 + sc1) + sh1).astype(BF16)
    z = jnp.dot(h, wcat_ref[...], preferred_element_type=F32)

    qn = _rms(z[:, C_QA:C_QA + Q_RANK], qag_ref[...]).astype(BF16)
    qa = jnp.dot(qn, wq_ref[...], preferred_element_type=F32)
    tq8 = jnp.concatenate([tq_ref[...]] * HEADS, axis=-1)
    q_ref[...] = (qa * tq8 * (ATTN_SCALE * math.log2(math.e))).astype(BF16)

    ckv = _rms(z[:, C_KVA:C_KVA + KV_RANK], kvag_ref[...])
    kp_rot = z[:, C_KP:C_KP + HEAD_PAD] * ck_ref[...] + z[:, C_KPS:C_KPS + HEAD_PAD] * sk_ref[...]
    ckv_b = ckv.astype(BF16)
    kn = jnp.dot(ckv_b, wkb_ref[...], preferred_element_type=F32)
    keys = (kn + jnp.concatenate([kp_rot] * HEADS, axis=-1)).astype(BF16)
    k_ref[...] = keys
    kn2_ref[...] = _max_key_norm2(keys)
    vt = lax.dot_general(wvt_ref[...], ckv_b, (((1,), (1,)), ((), ())), preferred_element_type=F32)
    vt_ref[...] = (vt + aug_ref[...]).astype(BF16)
    if is_ctx:
        seqs, n_ab = ckv_ref.shape[0], ckv_ref.shape[1]
        for slot in range(n_ab):
            if slot == layer:
                ckv_ref[:, slot] = ckv.reshape(seqs, CTX_SEQ_LEN, KV_RANK)
                kpe_ref[:, slot] = z[:, C_KO:C_KO + ROPE].reshape(seqs, CTX_SEQ_LEN, ROPE)
            elif prev_refs:
                ckv_ref[:, slot] = prev_refs[0][:, slot]
                kpe_ref[:, slot] = prev_refs[1][:, slot]
            else:
                ckv_ref[:, slot] = jnp.zeros((seqs, CTX_SEQ_LEN, KV_RANK), F32)
                kpe_ref[:, slot] = jnp.zeros((seqs, CTX_SEQ_LEN, ROPE), F32)

    u = jax.nn.gelu(z[:, C_U:C_U + GM_W])
    vv = jax.nn.gelu(z[:, C_V:C_V + GM_W])
    vn = _rms(vv, vg_ref[...])
    lane = lax.broadcasted_iota(jnp.int32, (GM_CHUNK, LANES), 1)
    lo = lane < GM_W // GM_GROUPS
    for c in range(tm // GM_CHUNK):
        rows = slice(c * GM_CHUNK, (c + 1) * GM_CHUNK)
        for j in range(GM_W // LANES):
            cols = slice(j * LANES, (j + 1) * LANES)
            blk = vn[rows, cols]
            rhs = jnp.concatenate([jnp.where(lo, blk, 0.0), jnp.where(lo, 0.0, blk)], axis=0).astype(BF16)
            s = jnp.dot(ws_ref[j], rhs, preferred_element_type=F32) + bs_ref[:, cols]
            gm_ref[rows, cols] = (u[rows, cols] * s).astype(BF16)


def _in_proj(x, mod_l, w, tabs, *, layer, ckv_prev=None, kpe_prev=None, lat_shape=None):
    n = x.shape[0]
    is_ctx = lat_shape is None
    row = lambda i: (i, 0)
    id_blk = tabs[0].shape[0] // TM - 1
    if is_ctx:
        tpb = None
        tab_idx = lambda i: (id_blk, 0)
        vt_shape, vt_spec = (VT_ROWS, n), pl.BlockSpec((VT_ROWS, TM), lambda i: (0, i))
    else:
        dec_batch, dec_seq = lat_shape
        tpb = dec_seq // TM
        tab_idx = lambda i: (i % tpb, 0)
        vt_shape = (dec_batch, VT_ROWS, dec_seq)
        vt_spec = pl.BlockSpec((None, VT_ROWS, TM), lambda i: (i // tpb, 0, i % tpb))
    w_arrs, w_specs = _weights(w, ["n1g", "wcat", "qag", "kvag", "wq", "wkb", "wvt", "aug", "vg", "ws", "bs"])
    in_specs = [pl.BlockSpec((TM, D), row), _mod_spec(tpb)] + w_specs + [pl.BlockSpec((TM, HEAD_PAD), tab_idx)] * 3
    args = [x, mod_l, *w_arrs, *tabs]
    out_shape = [jax.ShapeDtypeStruct((n, QK_W), BF16), jax.ShapeDtypeStruct((n, QK_W), BF16),
                 jax.ShapeDtypeStruct(vt_shape, BF16), jax.ShapeDtypeStruct((n, GM_W), BF16),
                 jax.ShapeDtypeStruct((n // TM, HEADS, LANES), F32)]
    out_specs = [pl.BlockSpec((TM, QK_W), row), pl.BlockSpec((TM, QK_W), row), vt_spec,
                 pl.BlockSpec((TM, GM_W), row), pl.BlockSpec((None, HEADS, LANES), lambda i: (i, 0, 0))]
    aliases = {}
    n_prev = 0
    if is_ctx:
        seqs = TM // CTX_SEQ_LEN
        batch = n // CTX_SEQ_LEN
        n_ab = DEPTH // 2
        cache_specs = [pl.BlockSpec((seqs, n_ab, CTX_SEQ_LEN, KV_RANK), lambda i: (i, 0, 0, 0)),
                       pl.BlockSpec((seqs, n_ab, CTX_SEQ_LEN, ROPE), lambda i: (i, 0, 0, 0))]
        out_shape += [jax.ShapeDtypeStruct((batch, n_ab, CTX_SEQ_LEN, KV_RANK), F32),
                      jax.ShapeDtypeStruct((batch, n_ab, CTX_SEQ_LEN, ROPE), F32)]
        out_specs += cache_specs
        if ckv_prev is not None:
            in_specs += cache_specs
            args += [ckv_prev, kpe_prev]
            aliases = {len(args) - 2: 5, len(args) - 1: 6}
            n_prev = 2

    return pl.pallas_call(
        functools.partial(_in_kernel, is_ctx=is_ctx, layer=layer, n_prev=n_prev),
        grid=(n // TM,),
        in_specs=in_specs,
        out_specs=out_specs,
        out_shape=out_shape,
        input_output_aliases=aliases,
        compiler_params=_params(1),
        name="in_proj_ctx" if is_ctx else "in_proj_lat",
    )(*args)


def _cache_kernel(ckv_ref, kpe_ref, wkb_ref, wvt_ref, aug_ref, place_ref, k_ref, vt_ref, kn2_ref):
    ckv_b = ckv_ref[...].astype(BF16)
    kn = jnp.dot(ckv_b, wkb_ref[...], preferred_element_type=F32)
    kp = jnp.dot(kpe_ref[...].astype(BF16), place_ref[...], preferred_element_type=F32)
    keys = (kn + jnp.concatenate([kp] * HEADS, axis=-1)).astype(BF16)
    k_ref[...] = keys
    kn2_ref[...] = _max_key_norm2(keys)
    vt = lax.dot_general(wvt_ref[...], ckv_b, (((1,), (1,)), ((), ())), preferred_element_type=F32)
    vt_ref[...] = (vt + aug_ref[...]).astype(BF16)


def _cache_keys(cache_ckv, cache_kpe, w, place, *, layer):
    dec_batch, _, past, _ = cache_ckv.shape
    w_arrs, w_specs = _weights(dict(w, place=place), ["wkb", "wvt", "aug", "place"])
    return pl.pallas_call(
        _cache_kernel,
        grid=(dec_batch,),
        in_specs=[
            pl.BlockSpec((None, None, past, KV_RANK), lambda b: (b, layer, 0, 0)),
            pl.BlockSpec((None, None, past, ROPE), lambda b: (b, layer, 0, 0)),
            *w_specs,
        ],
        out_specs=[pl.BlockSpec((None, past, QK_W), lambda b: (b, 0, 0)),
                   pl.BlockSpec((None, VT_ROWS, past), lambda b: (b, 0, 0)),
                   pl.BlockSpec((None, HEADS, LANES), lambda b: (b, 0, 0))],
        out_shape=(jax.ShapeDtypeStruct((dec_batch, past, QK_W), BF16),
                   jax.ShapeDtypeStruct((dec_batch, VT_ROWS, past), BF16),
                   jax.ShapeDtypeStruct((dec_batch, HEADS, LANES), F32)),
        compiler_params=_params(1),
        name="cache_keys",
    )(cache_ckv, cache_kpe, *w_arrs)


SAFE_DENOM = 2.0 ** -100
BOUND_SLACK = 1.02


def _attn_kernel(q_ref, *refs, n_src, segment):
    srcs = [(refs[2 * s], refs[2 * s + 1]) for s in range(n_src)]
    kmax_ref, o_ref, st_ref, p_ref = refs[2 * n_src:]

    def key_blocks(qt):
        blocks = []
        base = 0
        for k_ref, vt_ref in srcs:
            l_src = k_ref.shape[0] if segment is None else segment
            for start in range(0, l_src, MAX_KEY_BLK):
                size = min(MAX_KEY_BLK, l_src - start)
                if segment is None:
                    src_rows = slice(start, start + size)
                elif isinstance(qt, int):
                    src_rows = slice(qt * segment + start, qt * segment + start + size)
                else:
                    src_rows = pl.ds(pl.multiple_of(qt * segment + start, 2 * LANES), size)
                blocks.append((k_ref, vt_ref, src_rows, slice(base + start, base + start + size)))
            base += l_src
        return blocks

    n_qt = q_ref.shape[0] // TQ
    items = [(qt, h) for qt in range(n_qt) for h in range(HEADS)]
    n_items = len(items)
    ones = jnp.ones((8, HEAD_PAD), BF16)

    def finish(ot, qt, h, pair):
        pair[h] = ot[0:VDIM] * (1.0 / ot[VDIM:VDIM + 1])
        if h % 2 == 1:
            both = jnp.concatenate([pair.pop(h - 1), pair.pop(h)], axis=0)
            o_ref[qt * TQ:(qt + 1) * TQ, (h // 2) * LANES:(h // 2 + 1) * LANES] = both.T.astype(BF16)

    pair = {}
    lmin = None
    for step in range(n_items + 1):
        ia, ic = step, step - 1
        ot = None
        if ia < n_items:
            qt, h = items[ia]
            cols = slice(h * HEAD_PAD, (h + 1) * HEAD_PAD)
            qh = q_ref[qt * TQ:(qt + 1) * TQ, cols]
            qf = qh.astype(F32)
            qn2 = lax.dot_general(ones, (qf * qf).astype(BF16), (((1,), (1,)), ((), ())),
                                  preferred_element_type=F32)[0:1]
            km2 = kmax_ref[h:h + 1, :]
            shift = jnp.sqrt(qn2 * jnp.concatenate([km2] * (TQ // LANES), axis=-1)) * BOUND_SLACK
        for j in range(len(key_blocks(0))):
            if ia < n_items:
                k_ref, _, src_rows, rows = key_blocks(qt)[j]
                st = lax.dot_general(k_ref[src_rows, cols], qh, (((1,), (1,)), ((), ())),
                                     preferred_element_type=F32)
                p_ref[ia % 2, rows, :] = jnp.exp2(st - shift).astype(BF16)
            if 0 <= ic < n_items:
                qc, hc = items[ic]
                _, vt_ref, src_rows, rows = key_blocks(qc)[j]
                part = jnp.dot(vt_ref[hc * V_ROWS:(hc + 1) * V_ROWS, src_rows], p_ref[ic % 2, rows, :],
                               preferred_element_type=F32)
                ot = part if ot is None else ot + part
        if 0 <= ic < n_items:
            denom = ot[VDIM:VDIM + 1]
            lmin = denom if lmin is None else jnp.minimum(lmin, denom)
            finish(ot, *items[ic], pair)

    @pl.when(jnp.logical_not(jnp.min(lmin) >= SAFE_DENOM))
    def _():
        def redo(i, carry):
            qt = i // (HEADS // 2)
            hp = i % (HEADS // 2)
            q_rows = pl.ds(pl.multiple_of(qt * TQ, TQ), TQ)
            blocks = key_blocks(qt)
            outs = []
            for e in range(2):
                h = 2 * hp + e
                cols = pl.ds(pl.multiple_of(h * HEAD_PAD, HEAD_PAD), HEAD_PAD)
                qh = q_ref[q_rows, cols]
                for k_ref, _, src_rows, rows in blocks:
                    st_ref[0, rows, :] = lax.dot_general(k_ref[src_rows, cols], qh, (((1,), (1,)), ((), ())),
                                                         preferred_element_type=F32)
                m = jnp.max(st_ref[0], axis=0, keepdims=True)
                p_ref[0] = jnp.exp2(st_ref[0] - m).astype(BF16)
                ot = None
                v_rows = pl.ds(pl.multiple_of(h * V_ROWS, 16), V_ROWS)
                for _, vt_ref, src_rows, rows in blocks:
                    part = jnp.dot(vt_ref[v_rows, src_rows], p_ref[0, rows, :], preferred_element_type=F32)
                    ot = part if ot is None else ot + part
                outs.append(ot[0:VDIM] * (1.0 / ot[VDIM:VDIM + 1]))
            o_ref[q_rows, pl.ds(pl.multiple_of(hp * LANES, LANES), LANES)] = (
                jnp.concatenate(outs, axis=0).T.astype(BF16))
            return carry

        lax.fori_loop(0, n_qt * (HEADS // 2), redo, 0)


def _attention(q, sources, kmax2, *, batch, t_len, segment=None):
    tq = min(t_len, Q_TILES * TQ)
    nq = t_len // tq
    in_specs = [pl.BlockSpec((tq, QK_W), lambda b, qi: (b * nq + qi, 0))]
    args = [q]
    l_total = 0
    for k, vt in sources:
        if segment is None:
            l_len = k.shape[1]
            l_total += l_len
            in_specs += [pl.BlockSpec((None, l_len, QK_W), lambda b, qi: (b, 0, 0)),
                         pl.BlockSpec((None, VT_ROWS, l_len), lambda b, qi: (b, 0, 0))]
        else:
            assert segment == TQ and nq == 1
            l_total += segment
            in_specs += [pl.BlockSpec((tq, QK_W), lambda b, qi: (b, 0)),
                         pl.BlockSpec((VT_ROWS, tq), lambda b, qi: (0, b))]
        args += [k, vt]
    in_specs.append(pl.BlockSpec((None, HEADS, LANES), lambda b, qi: (b, 0, 0)))
    args.append(kmax2)
    return pl.pallas_call(
        functools.partial(_attn_kernel, n_src=len(sources), segment=segment),
        grid=(batch, nq),
        scratch_shapes=[pltpu.VMEM((1, l_total, TQ), F32), pltpu.VMEM((2, l_total, TQ), BF16)],
        in_specs=in_specs,
        out_specs=pl.BlockSpec((tq, HEADS * VDIM), lambda b, qi: (b * nq + qi, 0)),
        out_shape=jax.ShapeDtypeStruct((batch * t_len, HEADS * VDIM), BF16),
        compiler_params=_params(2),
        name="attention",
    )(*args)


def _mlp_tail(x1, mod_ref, n2g_ref, w1_ref, w2_ref, fg_ref, o_ref, final):
    sh2 = mod_ref[:, 3 * D:4 * D]
    sc2 = mod_ref[:, 4 * D:5 * D]
    g2 = mod_ref[:, 5 * D:6 * D]
    h = (_rms(x1, n2g_ref[...]) * (1.0 + sc2) + sh2).astype(BF16)
    acc = None
    for c in range(D_FF // TF):
        a = jnp.maximum(jnp.dot(h, w1_ref[:, c * TF:(c + 1) * TF], preferred_element_type=F32), 0.0)
        part = jnp.dot((a * a).astype(BF16), w2_ref[c * TF:(c + 1) * TF, :], preferred_element_type=F32)
        acc = part if acc is None else acc + part
    x2 = x1 + g2 * acc
    if final:
        x2 = _rms(x2, fg_ref[...])
    o_ref[...] = x2


def _ffn_ab_kernel(x_ref, attn_ref, gm_ref, mod_ref, wo_ref, n2g_ref, w1_ref, w2_ref, fg_ref, o_ref, *, final):
    g1 = mod_ref[:, 2 * D:3 * D]
    mix = (jnp.dot(attn_ref[...], wo_ref[0:HEADS * VDIM, :], preferred_element_type=F32)
           + jnp.dot(gm_ref[...], wo_ref[HEADS * VDIM:, :], preferred_element_type=F32))
    x1 = x_ref[...] + g1 * mix
    _mlp_tail(x1, mod_ref, n2g_ref, w1_ref, w2_ref, fg_ref, o_ref, final)


def _ffn_pool_kernel(x_ref, xp_ref, xn_ref, mod_ref, n1g_ref, band_ref, wp_ref, ps_ref, n2g_ref, w1_ref, w2_ref,
                     fg_ref, o_ref, *, final, tiles_per_seq):
    sh1 = mod_ref[:, 0 * D:1 * D]
    sc1 = mod_ref[:, 1 * D:2 * D]
    g1 = mod_ref[:, 2 * D:3 * D]
    n1g = n1g_ref[...]
    ada = lambda v: _rms(v, n1g) * (1.0 + sc1) + sh1
    x = x_ref[...]
    ht = ada(x)
    zero_halo = jnp.zeros((HALO, D), F32)
    if tiles_per_seq is None:
        seq_len, pos0 = SEG, 0
    else:
        t = pl.program_id(0) % tiles_per_seq
        seq_len, pos0 = tiles_per_seq * TM_F, t * TM_F
        tile_prev = jnp.where(t != 0, ada(xp_ref[...]), 0.0)
        tile_next = jnp.where(t != tiles_per_seq - 1, ada(xn_ref[...]), 0.0)
    n_seg = TM_F // SEG
    seg_rows = []
    for s in range(n_seg):
        seg = ht[s * SEG:(s + 1) * SEG]
        if tiles_per_seq is None:
            prev, nxt, base = zero_halo, zero_halo, 0
        else:
            prev = tile_prev if s == 0 else ht[s * SEG - HALO:s * SEG]
            nxt = tile_next if s == n_seg - 1 else ht[(s + 1) * SEG:(s + 1) * SEG + HALO]
            base = pos0 + s * SEG
        hs = jnp.concatenate([prev, seg, nxt], axis=0)
        hi = hs.astype(BF16)
        lo = (hs - hi.astype(F32)).astype(BF16)
        pos = base + lax.broadcasted_iota(jnp.int32, (SEG, POOL_GD), 0)
        parts = []
        for gi, w in enumerate(POOL_WINDOWS):
            half = w // 2
            cols = slice(gi * POOL_GD, (gi + 1) * POOL_GD)
            top = (jnp.dot(band_ref[gi, 0], hi[0:SEG, cols], preferred_element_type=F32)
                   + jnp.dot(band_ref[gi, 0], lo[0:SEG, cols], preferred_element_type=F32))
            bot = (jnp.dot(band_ref[gi, 1], hi[2 * HALO:, cols], preferred_element_type=F32)
                   + jnp.dot(band_ref[gi, 1], lo[2 * HALO:, cols], preferred_element_type=F32))
            tot = jnp.concatenate([top, bot], axis=0)
            cnt = (jnp.minimum(pos + half, seq_len) - jnp.maximum(pos - half, 0)).astype(F32)
            diff = (tot / cnt - seg[:, cols]).astype(BF16)
            parts.append(jnp.dot(diff, wp_ref[gi], preferred_element_type=F32))
        seg_rows.append(jnp.concatenate(parts, axis=-1))
    mix = jnp.concatenate(seg_rows, axis=0) * ps_ref[...]
    x1 = x + g1 * mix
    _mlp_tail(x1, mod_ref, n2g_ref, w1_ref, w2_ref, fg_ref, o_ref, final)


def _pool_bands():
    half_rows = SEG // 2
    t = np.arange(half_rows)[:, None]
    u = np.arange(SEG)[None, :]
    bands = np.zeros((len(POOL_WINDOWS), 2, half_rows, SEG), np.float32)
    for gi, w in enumerate(POOL_WINDOWS):
        half = w // 2
        d_top = (u - HALO) - t
        d_bot = (u + HALO) - (half_rows + t)
        bands[gi, 0] = (d_top >= -half) & (d_top <= half - 1)
        bands[gi, 1] = (d_bot >= -half) & (d_bot <= half - 1)
    return jnp.asarray(bands, BF16)


def _ffn_ab(x, attn, gm, mod_l, w, final, tiles_per_batch):
    n = x.shape[0]
    row = lambda i: (i, 0)
    w_arrs, w_specs = _weights(w, ["wo", "n2g", "w1", "w2", "fg"])
    return pl.pallas_call(
        functools.partial(_ffn_ab_kernel, final=final),
        grid=(n // TM_F,),
        in_specs=[
            pl.BlockSpec((TM_F, D), row),
            pl.BlockSpec((TM_F, HEADS * VDIM), row),
            pl.BlockSpec((TM_F, GM_W), row),
            _mod_spec(tiles_per_batch),
        ] + w_specs,
        out_specs=pl.BlockSpec((TM_F, D), row),
        out_shape=jax.ShapeDtypeStruct((n, D), F32),
        compiler_params=_params(1),
        name="ffn_ab",
    )(x, attn, gm, mod_l, *w_arrs)


def _ffn_pool(x, mod_l, w, final, tiles_per_batch):
    n = x.shape[0]
    row = lambda i: (i, 0)
    hb = TM_F // HALO
    w_arrs, w_specs = _weights(w, ["n1g", "bands", "wp", "ps", "n2g", "w1", "w2", "fg"])
    return pl.pallas_call(
        functools.partial(_ffn_pool_kernel, final=final, tiles_per_seq=tiles_per_batch),
        grid=(n // TM_F,),
        in_specs=[
            pl.BlockSpec((TM_F, D), row),
            pl.BlockSpec((HALO, D), lambda i: (jnp.maximum(i * hb - 1, 0), 0)),
            pl.BlockSpec((HALO, D), lambda i: (jnp.minimum((i + 1) * hb, n // HALO - 1), 0)),
            _mod_spec(tiles_per_batch),
        ] + w_specs,
        out_specs=pl.BlockSpec((TM_F, D), row),
        out_shape=jax.ShapeDtypeStruct((n, D), F32),
        compiler_params=_params(1),
        name="ffn_pool",
    )(x, x, x, mod_l, *w_arrs)


_SWAP = np.arange(ROPE).reshape(2, 2, ROPE // 4)[:, ::-1, :].reshape(ROPE)


def _rope_tables(t_len):
    rows = t_len // GRID_W
    row = jnp.repeat(jnp.arange(rows), GRID_W).astype(F32)
    col = jnp.tile(jnp.arange(GRID_W), rows).astype(F32)
    axis_dim = ROPE // 2
    inv = ROPE_BASE ** (-jnp.arange(0, axis_dim, 2, dtype=F32) / axis_dim)
    ang = jnp.stack([row[:, None] * inv, col[:, None] * inv], axis=1)
    cos = jnp.cos(ang)
    sin = jnp.sin(ang)
    c32 = jnp.stack([cos, cos], axis=2).reshape(t_len, ROPE)
    s32 = jnp.stack([-sin, sin], axis=2).reshape(t_len, ROPE)
    one = lambda r, c: jnp.ones((r, c), F32)
    zero = lambda r, c: jnp.zeros((r, c), F32)
    tq = jnp.concatenate([jnp.concatenate([one(t_len, NOPE), c32, s32], -1),
                          jnp.concatenate([one(TM, NOPE + ROPE), zero(TM, ROPE)], -1)], 0)
    ck = jnp.concatenate([jnp.concatenate([zero(t_len, NOPE), c32, c32], -1),
                          jnp.concatenate([zero(TM, NOPE), one(TM, 2 * ROPE)], -1)], 0)
    sk = jnp.concatenate([jnp.concatenate([zero(t_len, NOPE), s32, s32], -1), zero(TM, HEAD_PAD)], 0)
    return tq, ck, sk


def _prep_even_weights(w_in_ab, q_a_g, kv_a_g, w_q_b, w_kv_b, gmlp_v_g, w_spatial, b_spatial, w_out_ab):
    n_ab = w_in_ab.shape[0]
    kpe_off = Q_RANK + KV_RANK
    kp_w = w_in_ab[:, :, kpe_off:kpe_off + ROPE]
    kp_sw = kp_w[:, :, _SWAP]
    z = lambda c: jnp.zeros((n_ab, D, c), F32)
    wcat = jnp.concatenate(
        [w_in_ab[:, :, :kpe_off], w_in_ab[:, :, kpe_off + ROPE:],
         z(NOPE), kp_w, kp_w, z(NOPE), kp_sw, kp_sw, kp_w, z(HEAD_PAD - ROPE)], axis=2).astype(BF16)
    wq4 = w_q_b.reshape(n_ab, Q_RANK, HEADS, NOPE + ROPE)
    wq = jnp.concatenate([wq4, wq4[..., NOPE:][..., _SWAP]], axis=-1).reshape(n_ab, Q_RANK, QK_W).astype(BF16)
    wkv4 = w_kv_b.reshape(n_ab, KV_RANK, HEADS, NOPE + VDIM)
    wkb = jnp.pad(wkv4[..., :NOPE], ((0, 0), (0, 0), (0, 0), (0, HEAD_PAD - NOPE))).reshape(
        n_ab, KV_RANK, QK_W).astype(BF16)
    wvt = jnp.pad(wkv4[..., NOPE:].transpose(0, 2, 3, 1), ((0, 0), (0, 0), (0, V_ROWS - VDIM), (0, 0))).reshape(
        n_ab, VT_ROWS, KV_RANK).astype(BF16)
    ws = w_spatial.reshape(n_ab, GM_GROUPS // 2, 2, GM_CHUNK, GM_CHUNK).transpose(0, 1, 3, 2, 4).reshape(
        n_ab, GM_GROUPS // 2, GM_CHUNK, 2 * GM_CHUNK).astype(BF16)
    bs = jnp.repeat(b_spatial.transpose(0, 2, 1), GM_W // GM_GROUPS, axis=2)
    aug = np.zeros((VT_ROWS, 1), np.float32)
    aug[VDIM::V_ROWS] = 1.0
    return dict(wcat=wcat, wq=wq, wkb=wkb, wvt=wvt, ws=ws, bs=bs, aug=jnp.asarray(aug),
                qag=q_a_g[:, None], kvag=kv_a_g[:, None], vg=gmlp_v_g[:, None], wo=w_out_ab.astype(BF16))


def kernel(x_prompt, x_sample, cache_ckv, cache_kpe, c, c_ctx, w_mod, b_mod, norm1_g, norm2_g, w_in_ab, q_a_g,
           kv_a_g, w_q_b, w_kv_b, gmlp_v_g, w_spatial, b_spatial, w_out_ab, w_pool, pool_scale, w_ff1, w_ff2,
           final_g):
    batch, seq, _ = x_prompt.shape
    dec_batch, dec_seq, _ = x_sample.shape
    past = cache_ckv.shape[2]
    assert seq == CTX_SEQ_LEN and dec_seq % TM_F == 0 and dec_seq % TM == 0 and past % (2 * LANES) == 0
    lat_tiles = dec_seq // TM_F

    xc = x_prompt.reshape(batch * seq, D)
    xl = x_sample.reshape(dec_batch * dec_seq, D)
    cond8 = jnp.concatenate([c_ctx[None], c, jnp.zeros((MOD_ROWS - 1 - dec_batch, D), F32)], axis=0)
    mod = _modulation(cond8, w_mod, b_mod).reshape(DEPTH, MOD_ROWS, 1, N_MOD * D)
    tabs = _rope_tables(dec_seq)
    bands = _pool_bands()
    place = np.zeros((ROPE, HEAD_PAD), np.float32)
    place[np.arange(ROPE), NOPE + np.arange(ROPE)] = 1.0
    place[np.arange(ROPE), NOPE + ROPE + np.arange(ROPE)] = 1.0
    place = jnp.asarray(place, BF16)

    ev = _prep_even_weights(w_in_ab, q_a_g, kv_a_g, w_q_b, w_kv_b, gmlp_v_g, w_spatial, b_spatial, w_out_ab)
    w1 = w_ff1.astype(BF16)
    w2 = w_ff2.astype(BF16)
    wp = w_pool.astype(BF16)
    n1g = norm1_g[:, None]
    n2g = norm2_g[:, None]
    ps = pool_scale[:, None]

    new_ckv = new_kpe = None
    for l in range(DEPTH):
        final = l == DEPTH - 1
        i = l // 2
        w = dict(n1g=(n1g, l), n2g=(n2g, l), w1=(w1, l), w2=(w2, l), fg=final_g[None], bands=bands)
        if l % 2 == 0:
            w.update({name: (val, i) for name, val in ev.items() if name != "aug"}, aug=ev["aug"])
            qc, kc, vtc, gmc, kn2_ctx, new_ckv, new_kpe = _in_proj(
                xc, mod[l], w, tabs, layer=i, ckv_prev=new_ckv, kpe_prev=new_kpe)
            ql, kl, vtl, gml, kn2_l = _in_proj(xl, mod[l], w, tabs, layer=i, lat_shape=(dec_batch, dec_seq))
            k_cache, vt_cache, kn2_c = _cache_keys(cache_ckv, cache_kpe, w, place, layer=i)
            kmax2 = jnp.maximum(kn2_l.reshape(dec_batch, -1, HEADS, LANES).max(axis=1), kn2_c)
            attn_c = _attention(qc, [(kc, vtc)], kn2_ctx, batch=batch * seq // TM, t_len=TM, segment=seq)
            attn_l = _attention(ql, [(k_cache, vt_cache), (kl.reshape(dec_batch, dec_seq, QK_W), vtl)], kmax2,
                                batch=dec_batch, t_len=dec_seq)
            xc = _ffn_ab(xc, attn_c, gmc, mod[l], w, final, None)
            xl = _ffn_ab(xl, attn_l, gml, mod[l], w, final, lat_tiles)
        else:
            w.update(wp=(wp, i), ps=(ps, i))
            xc = _ffn_pool(xc, mod[l], w, final, None)
            xl = _ffn_pool(xl, mod[l], w, final, lat_tiles)

    return xc.reshape(batch, seq, D), xl.reshape(dec_batch, dec_seq, D), new_ckv, new_kpe
```

```python
import functools
import math

import numpy as np
import jax
import jax.numpy as jnp
from jax import lax
from jax.experimental import pallas as pl
from jax.experimental.pallas import tpu as pltpu

F32 = jnp.float32
BF16 = jnp.bfloat16

D = 1024
DEPTH = 4
N_MOD = 6
HEADS = 8
NOPE = 64
ROPE = 32
VDIM = 64
Q_RANK = 384
KV_RANK = 256
GRID_W = 64
ROPE_BASE = 10000.0
ATTN_SCALE = (NOPE + ROPE) ** -0.5
GM_W = 512
GM_GROUPS = 8
GM_CHUNK = 128
POOL_WINDOWS = (2, 4, 8, 16)
POOL_GD = D // len(POOL_WINDOWS)
D_FF = 4 * D
EPS = 1e-6
CTX_SEQ_LEN = 256

LANES = 128
HEAD_PAD = LANES
QK_W = HEADS * HEAD_PAD
V_ROWS = VDIM + 16
VT_ROWS = HEADS * V_ROWS
MOD_ROWS = 8
HALO = 8
SEG = 256

C_QA = 0
C_KVA = C_QA + Q_RANK
C_U = C_KVA + KV_RANK
C_V = C_U + GM_W
C_KP = C_V + GM_W
C_KPS = C_KP + HEAD_PAD
C_KO = C_KPS + HEAD_PAD
IN_W = C_KO + HEAD_PAD

TM = 1024
TM_F = 1024
TF = 1024
TQ = 256
Q_TILES = 4
MAX_KEY_BLK = 2048
VMEM_LIMIT = 56 * 1024 * 1024


def _rms(x, g):
    return x * lax.rsqrt(jnp.mean(x * x, axis=-1, keepdims=True) + EPS) * g


def _const_spec(shape):
    nd = len(shape)
    return pl.BlockSpec(shape, lambda *_: (0,) * nd)


def _weights(w, names):
    arrs, specs = [], []
    for name in names:
        entry = w[name]
        if isinstance(entry, tuple):
            arr, layer = entry
            shape = arr.shape[1:]
            specs.append(pl.BlockSpec((None,) + shape, lambda *_, layer=layer, nd=len(shape): (layer,) + (0,) * nd,
                                      pipeline_mode=pl.Buffered(1)))
        else:
            arr = entry
            specs.append(pl.BlockSpec(arr.shape, lambda *_, nd=arr.ndim: (0,) * nd, pipeline_mode=pl.Buffered(1)))
        arrs.append(arr)
    return arrs, specs


def _params(n_axes, flags=None):
    return pltpu.CompilerParams(dimension_semantics=("arbitrary",) * n_axes, vmem_limit_bytes=VMEM_LIMIT,
                                flags=flags)


def _mod_kernel(cond_ref, w_ref, b_ref, o_ref):
    c = cond_ref[...]
    s = (c / (1.0 + jnp.exp(-c))).astype(BF16)
    w = w_ref[...].astype(BF16)
    o_ref[...] = jnp.dot(s, w, preferred_element_type=F32) + b_ref[...]


def _modulation(cond8, w_mod, b_mod):
    tn = 1536
    nw = N_MOD * D
    return pl.pallas_call(
        _mod_kernel,
        grid=(DEPTH, nw // tn),
        in_specs=[
            pl.BlockSpec((MOD_ROWS, D), lambda l, j: (0, 0)),
            pl.BlockSpec((None, D, tn), lambda l, j: (l, 0, j)),
            pl.BlockSpec((None, 1, tn), lambda l, j: (l, 0, j)),
        ],
        out_specs=pl.BlockSpec((None, MOD_ROWS, tn), lambda l, j: (l, 0, j)),
        out_shape=jax.ShapeDtypeStruct((DEPTH, MOD_ROWS, nw), F32),
        compiler_params=_params(2),
        name="modulation",
    )(cond8, w_mod, b_mod.reshape(DEPTH, 1, nw))


def _mod_spec(tiles_per_batch):
    if tiles_per_batch is None:
        return pl.BlockSpec((None, 1, N_MOD * D), lambda i: (0, 0, 0))
    return pl.BlockSpec((None, 1, N_MOD * D), lambda i: (1 + i // tiles_per_batch, 0, 0))


def _max_key_norm2(keys):
    kf = keys.astype(F32)
    ksq = kf * kf
    rows = []
    for h in range(HEADS):
        norm2 = jnp.sum(ksq[:, h * HEAD_PAD:(h + 1) * HEAD_PAD], axis=-1, keepdims=True)
        rows.append(jnp.broadcast_to(jnp.max(norm2, axis=0, keepdims=True), (1, LANES)))
    return jnp.concatenate(rows, axis=0)


def _in_kernel(x_ref, mod_ref, n1g_ref, wcat_ref, qag_ref, kvag_ref, wq_ref, wkb_ref, wvt_ref, aug_ref,
               vg_ref, ws_ref, bs_ref, tq_ref, ck_ref, sk_ref, *rest, is_ctx, layer, n_prev):
    prev_refs, out_refs = rest[:n_prev], rest[n_prev:]
    q_ref, k_ref, vt_ref, gm_ref, kn2_ref = out_refs[:5]
    if is_ctx:
        ckv_ref, kpe_ref = out_refs[5:]
    tm = x_ref.shape[0]
    x = x_ref[...]
    sh1 = mod_ref[:, 0 * D:1 * D]
    sc1 = mod_ref[:, 1 * D:2 * D]
    h = (_rms(x, n1g_ref[...]) * (1.0 + sc1) + sh1).astype(BF16)
    z = jnp.dot(h, wcat_ref[...], preferred_element_type=F32)

    qn = _rms(z[:, C_QA:C_QA + Q_RANK], qag_ref[...]).astype(BF16)
    qa = jnp.dot(qn, wq_ref[...], preferred_element_type=F32)
    tq8 = jnp.concatenate([tq_ref[...]] * HEADS, axis=-1)
    q_ref[...] = (qa * tq8 * (ATTN_SCALE * math.log2(math.e))).astype(BF16)

    ckv = _rms(z[:, C_KVA:C_KVA + KV_RANK], kvag_ref[...])
    kp_rot = z[:, C_KP:C_KP + HEAD_PAD] * ck_ref[...] + z[:, C_KPS:C_KPS + HEAD_PAD] * sk_ref[...]
    ckv_b = ckv.astype(BF16)
    kn = jnp.dot(ckv_b, wkb_ref[...], preferred_element_type=F32)
    keys = (kn + jnp.concatenate([kp_rot] * HEADS, axis=-1)).astype(BF16)
    k_ref[...] = keys
    kn2_ref[...] = _max_key_norm2(keys)
    vt = lax.dot_general(wvt_ref[...], ckv_b, (((1,), (1,)), ((), ())), preferred_element_type=F32)
    vt_ref[...] = (vt + aug_ref[...]).astype(BF16)
    if is_ctx:
        seqs, n_ab = ckv_ref.shape[0], ckv_ref.shape[1]
        for slot in range(n_ab):
            if slot == layer:
                ckv_ref[:, slot] = ckv.reshape(seqs, CTX_SEQ_LEN, KV_RANK)
                kpe_ref[:, slot] = z[:, C_KO:C_KO + ROPE].reshape(seqs, CTX_SEQ_LEN, ROPE)
            elif prev_refs:
                ckv_ref[:, slot] = prev_refs[0][:, slot]
                kpe_ref[:, slot] = prev_refs[1][:, slot]
            else:
                ckv_ref[:, slot] = jnp.zeros((seqs, CTX_SEQ_LEN, KV_RANK), F32)
                kpe_ref[:, slot] = jnp.zeros((seqs, CTX_SEQ_LEN, ROPE), F32)

    u = jax.nn.gelu(z[:, C_U:C_U + GM_W])
    vv = jax.nn.gelu(z[:, C_V:C_V + GM_W])
    vn = _rms(vv, vg_ref[...])
    lane = lax.broadcasted_iota(jnp.int32, (GM_CHUNK, LANES), 1)
    lo = lane < GM_W // GM_GROUPS
    for c in range(tm // GM_CHUNK):
        rows = slice(c * GM_CHUNK, (c + 1) * GM_CHUNK)
        for j in range(GM_W // LANES):
            cols = slice(j * LANES, (j + 1) * LANES)
            blk = vn[rows, cols]
            rhs = jnp.concatenate([jnp.where(lo, blk, 0.0), jnp.where(lo, 0.0, blk)], axis=0).astype(BF16)
            s = jnp.dot(ws_ref[j], rhs, preferred_element_type=F32) + bs_ref[:, cols]
            gm_ref[rows, cols] = (u[rows, cols] * s).astype(BF16)


def _in_proj(x, mod_l, w, tabs, *, layer, ckv_prev=None, kpe_prev=None, lat_shape=None):
    n = x.shape[0]
    is_ctx = lat_shape is None
    row = lambda i: (i, 0)
    id_blk = tabs[0].shape[0] // TM - 1
    if is_ctx:
        tpb = None
        tab_idx = lambda i: (id_blk, 0)
        vt_shape, vt_spec = (VT_ROWS, n), pl.BlockSpec((VT_ROWS, TM), lambda i: (0, i))
    else:
        dec_batch, dec_seq = lat_shape
        tpb = dec_seq // TM
        tab_idx = lambda i: (i % tpb, 0)
        vt_shape = (dec_batch, VT_ROWS, dec_seq)
        vt_spec = pl.BlockSpec((None, VT_ROWS, TM), lambda i: (i // tpb, 0, i % tpb))
    w_arrs, w_specs = _weights(w, ["n1g", "wcat", "qag", "kvag", "wq", "wkb", "wvt", "aug", "vg", "ws", "bs"])
    in_specs = [pl.BlockSpec((TM, D), row), _mod_spec(tpb)] + w_specs + [pl.BlockSpec((TM, HEAD_PAD), tab_idx)] * 3
    args = [x, mod_l, *w_arrs, *tabs]
    out_shape = [jax.ShapeDtypeStruct((n, QK_W), BF16), jax.ShapeDtypeStruct((n, QK_W), BF16),
                 jax.ShapeDtypeStruct(vt_shape, BF16), jax.ShapeDtypeStruct((n, GM_W), BF16),
                 jax.ShapeDtypeStruct((n // TM, HEADS, LANES), F32)]
    out_specs = [pl.BlockSpec((TM, QK_W), row), pl.BlockSpec((TM, QK_W), row), vt_spec,
                 pl.BlockSpec((TM, GM_W), row), pl.BlockSpec((None, HEADS, LANES), lambda i: (i, 0, 0))]
    aliases = {}
    n_prev = 0
    if is_ctx:
        seqs = TM // CTX_SEQ_LEN
        batch = n // CTX_SEQ_LEN
        n_ab = DEPTH // 2
        cache_specs = [pl.BlockSpec((seqs, n_ab, CTX_SEQ_LEN, KV_RANK), lambda i: (i, 0, 0, 0)),
                       pl.BlockSpec((seqs, n_ab, CTX_SEQ_LEN, ROPE), lambda i: (i, 0, 0, 0))]
        out_shape += [jax.ShapeDtypeStruct((batch, n_ab, CTX_SEQ_LEN, KV_RANK), F32),
                      jax.ShapeDtypeStruct((batch, n_ab, CTX_SEQ_LEN, ROPE), F32)]
        out_specs += cache_specs
        if ckv_prev is not None:
            in_specs += cache_specs
            args += [ckv_prev, kpe_prev]
            aliases = {len(args) - 2: 5, len(args) - 1: 6}
            n_prev = 2

    return pl.pallas_call(
        functools.partial(_in_kernel, is_ctx=is_ctx, layer=layer, n_prev=n_prev),
        grid=(n // TM,),
        in_specs=in_specs,
        out_specs=out_specs,
        out_shape=out_shape,
        input_output_aliases=aliases,
        compiler_params=_params(1),
        name="in_proj_ctx" if is_ctx else "in_proj_lat",
    )(*args)


def _cache_kernel(ckv_ref, kpe_ref, wkb_ref, wvt_ref, aug_ref, place_ref, k_ref, vt_ref, kn2_ref):
    ckv_b = ckv_ref[...].astype(BF16)
    kn = jnp.dot(ckv_b, wkb_ref[...], preferred_element_type=F32)
    kp = jnp.dot(kpe_ref[...].astype(BF16), place_ref[...], preferred_element_type=F32)
    keys = (kn + jnp.concatenate([kp] * HEADS, axis=-1)).astype(BF16)
    k_ref[...] = keys
    kn2_ref[...] = _max_key_norm2(keys)
    vt = lax.dot_general(wvt_ref[...], ckv_b, (((1,), (1,)), ((), ())), preferred_element_type=F32)
    vt_ref[...] = (vt + aug_ref[...]).astype(BF16)


def _cache_keys(cache_ckv, cache_kpe, w, place, *, layer):
    dec_batch, _, past, _ = cache_ckv.shape
    w_arrs, w_specs = _weights(dict(w, place=place), ["wkb", "wvt", "aug", "place"])
    return pl.pallas_call(
        _cache_kernel,
        grid=(dec_batch,),
        in_specs=[
            pl.BlockSpec((None, None, past, KV_RANK), lambda b: (b, layer, 0, 0)),
            pl.BlockSpec((None, None, past, ROPE), lambda b: (b, layer, 0, 0)),
            *w_specs,
        ],
        out_specs=[pl.BlockSpec((None, past, QK_W), lambda b: (b, 0, 0)),
                   pl.BlockSpec((None, VT_ROWS, past), lambda b: (b, 0, 0)),
                   pl.BlockSpec((None, HEADS, LANES), lambda b: (b, 0, 0))],
        out_shape=(jax.ShapeDtypeStruct((dec_batch, past, QK_W), BF16),
                   jax.ShapeDtypeStruct((dec_batch, VT_ROWS, past), BF16),
                   jax.ShapeDtypeStruct((dec_batch, HEADS, LANES), F32)),
        compiler_params=_params(1),
        name="cache_keys",
    )(cache_ckv, cache_kpe, *w_arrs)


SAFE_DENOM = 2.0 ** -100
BOUND_SLACK = 1.02


def _attn_kernel(q_ref, *refs, n_src, segment):
    srcs = [(refs[2 * s], refs[2 * s + 1]) for s in range(n_src)]
    kmax_ref, o_ref, st_ref, p_ref = refs[2 * n_src:]

    def key_blocks(qt):
        blocks = []
        base = 0
        for k_ref, vt_ref in srcs:
            l_src = k_ref.shape[0] if segment is None else segment
            for start in range(0, l_src, MAX_KEY_BLK):
                size = min(MAX_KEY_BLK, l_src - start)
                if segment is None:
                    src_rows = slice(start, start + size)
                elif isinstance(qt, int):
                    src_rows = slice(qt * segment + start, qt * segment + start + size)
                else:
                    src_rows = pl.ds(pl.multiple_of(qt * segment + start, 2 * LANES), size)
                blocks.append((k_ref, vt_ref, src_rows, slice(base + start, base + start + size)))
            base += l_src
        return blocks

    n_qt = q_ref.shape[0] // TQ
    items = [(qt, h) for qt in range(n_qt) for h in range(HEADS)]
    n_items = len(items)
    ones = jnp.ones((8, HEAD_PAD), BF16)

    def finish(ot, qt, h, pair):
        pair[h] = ot[0:VDIM] * (1.0 / ot[VDIM:VDIM + 1])
        if h % 2 == 1:
            both = jnp.concatenate([pair.pop(h - 1), pair.pop(h)], axis=0)
            o_ref[qt * TQ:(qt + 1) * TQ, (h // 2) * LANES:(h // 2 + 1) * LANES] = both.T.astype(BF16)

    pair = {}
    bad = None
    for step in range(n_items + 1):
        ia, ic = step, step - 1
        ot = None
        if ia < n_items:
            qt, h = items[ia]
            cols = slice(h * HEAD_PAD, (h + 1) * HEAD_PAD)
            qh = q_ref[qt * TQ:(qt + 1) * TQ, cols]
            qf = qh.astype(F32)
            qn2 = lax.dot_general(ones, (qf * qf).astype(BF16), (((1,), (1,)), ((), ())),
                                  preferred_element_type=F32)[0:1]
            km2 = kmax_ref[h:h + 1, :]
            shift = jnp.sqrt(qn2 * jnp.concatenate([km2] * (TQ // LANES), axis=-1)) * BOUND_SLACK
        for j in range(len(key_blocks(0))):
            if ia < n_items:
                k_ref, _, src_rows, rows = key_blocks(qt)[j]
                st = lax.dot_general(k_ref[src_rows, cols], qh, (((1,), (1,)), ((), ())),
                                     preferred_element_type=F32)
                p_ref[ia % 2, rows, :] = jnp.exp2(st - shift).astype(BF16)
            if 0 <= ic < n_items:
                qc, hc = items[ic]
                _, vt_ref, src_rows, rows = key_blocks(qc)[j]
                part = jnp.dot(vt_ref[hc * V_ROWS:(hc + 1) * V_ROWS, src_rows], p_ref[ic % 2, rows, :],
                               preferred_element_type=F32)
                ot = part if ot is None else ot + part
        if 0 <= ic < n_items:
            unsafe = jnp.where(ot[VDIM:VDIM + 1] >= SAFE_DENOM, 0.0, 1.0)
            bad = unsafe if bad is None else jnp.maximum(bad, unsafe)
            finish(ot, *items[ic], pair)

    @pl.when(jnp.max(bad) > 0.0)
    def _():
        def redo(i, carry):
            qt = i // (HEADS // 2)
            hp = i % (HEADS // 2)
            q_rows = pl.ds(pl.multiple_of(qt * TQ, TQ), TQ)
            blocks = key_blocks(qt)
            outs = []
            for e in range(2):
                h = 2 * hp + e
                cols = pl.ds(pl.multiple_of(h * HEAD_PAD, HEAD_PAD), HEAD_PAD)
                qh = q_ref[q_rows, cols]
                for k_ref, _, src_rows, rows in blocks:
                    st_ref[0, rows, :] = lax.dot_general(k_ref[src_rows, cols], qh, (((1,), (1,)), ((), ())),
                                                         preferred_element_type=F32)
                m = jnp.max(st_ref[0], axis=0, keepdims=True)
                p_ref[0] = jnp.exp2(st_ref[0] - m).astype(BF16)
                ot = None
                v_rows = pl.ds(pl.multiple_of(h * V_ROWS, 16), V_ROWS)
                for _, vt_ref, src_rows, rows in blocks:
                    part = jnp.dot(vt_ref[v_rows, src_rows], p_ref[0, rows, :], preferred_element_type=F32)
                    ot = part if ot is None else ot + part
                outs.append(ot[0:VDIM] * (1.0 / ot[VDIM:VDIM + 1]))
            o_ref[q_rows, pl.ds(pl.multiple_of(hp * LANES, LANES), LANES)] = (
                jnp.concatenate(outs, axis=0).T.astype(BF16))
            return carry

        lax.fori_loop(0, n_qt * (HEADS // 2), redo, 0)


def _attention(q, sources, kmax2, *, batch, t_len, segment=None):
    tq = min(t_len, Q_TILES * TQ)
    nq = t_len // tq
    in_specs = [pl.BlockSpec((tq, QK_W), lambda b, qi: (b * nq + qi, 0))]
    args = [q]
    l_total = 0
    for k, vt in sources:
        if segment is None:
            l_len = k.shape[1]
            l_total += l_len
            in_specs += [pl.BlockSpec((None, l_len, QK_W), lambda b, qi: (b, 0, 0)),
                         pl.BlockSpec((None, VT_ROWS, l_len), lambda b, qi: (b, 0, 0))]
        else:
            assert segment == TQ and nq == 1
            l_total += segment
            in_specs += [pl.BlockSpec((tq, QK_W), lambda b, qi: (b, 0)),
                         pl.BlockSpec((VT_ROWS, tq), lambda b, qi: (0, b))]
        args += [k, vt]
    in_specs.append(pl.BlockSpec((None, HEADS, LANES), lambda b, qi: (b, 0, 0)))
    args.append(kmax2)
    return pl.pallas_call(
        functools.partial(_attn_kernel, n_src=len(sources), segment=segment),
        grid=(batch, nq),
        scratch_shapes=[pltpu.VMEM((1, l_total, TQ), F32), pltpu.VMEM((2, l_total, TQ), BF16)],
        in_specs=in_specs,
        out_specs=pl.BlockSpec((tq, HEADS * VDIM), lambda b, qi: (b * nq + qi, 0)),
        out_shape=jax.ShapeDtypeStruct((batch * t_len, HEADS * VDIM), BF16),
        compiler_params=_params(2),
        name="attention",
    )(*args)


def _mlp_tail(x1, mod_ref, n2g_ref, w1_ref, w2_ref, fg_ref, o_ref, final):
    sh2 = mod_ref[:, 3 * D:4 * D]
    sc2 = mod_ref[:, 4 * D:5 * D]
    g2 = mod_ref[:, 5 * D:6 * D]
    h = (_rms(x1, n2g_ref[...]) * (1.0 + sc2) + sh2).astype(BF16)
    acc = None
    for c in range(D_FF // TF):
        a = jnp.maximum(jnp.dot(h, w1_ref[:, c * TF:(c + 1) * TF], preferred_element_type=F32), 0.0)
        part = jnp.dot((a * a).astype(BF16), w2_ref[c * TF:(c + 1) * TF, :], preferred_element_type=F32)
        acc = part if acc is None else acc + part
    x2 = x1 + g2 * acc
    if final:
        x2 = _rms(x2, fg_ref[...])
    o_ref[...] = x2


def _ffn_ab_kernel(x_ref, attn_ref, gm_ref, mod_ref, wo_ref, n2g_ref, w1_ref, w2_ref, fg_ref, o_ref, *, final):
    g1 = mod_ref[:, 2 * D:3 * D]
    mix = (jnp.dot(attn_ref[...], wo_ref[0:HEADS * VDIM, :], preferred_element_type=F32)
           + jnp.dot(gm_ref[...], wo_ref[HEADS * VDIM:, :], preferred_element_type=F32))
    x1 = x_ref[...] + g1 * mix
    _mlp_tail(x1, mod_ref, n2g_ref, w1_ref, w2_ref, fg_ref, o_ref, final)


def _ffn_pool_kernel(x_ref, xp_ref, xn_ref, mod_ref, n1g_ref, band_ref, wp_ref, ps_ref, n2g_ref, w1_ref, w2_ref,
                     fg_ref, o_ref, *, final, tiles_per_seq):
    sh1 = mod_ref[:, 0 * D:1 * D]
    sc1 = mod_ref[:, 1 * D:2 * D]
    g1 = mod_ref[:, 2 * D:3 * D]
    n1g = n1g_ref[...]
    ada = lambda v: _rms(v, n1g) * (1.0 + sc1) + sh1
    x = x_ref[...]
    ht = ada(x)
    zero_halo = jnp.zeros((HALO, D), F32)
    if tiles_per_seq is None:
        seq_len, pos0 = SEG, 0
    else:
        t = pl.program_id(0) % tiles_per_seq
        seq_len, pos0 = tiles_per_seq * TM_F, t * TM_F
        tile_prev = jnp.where(t != 0, ada(xp_ref[...]), 0.0)
        tile_next = jnp.where(t != tiles_per_seq - 1, ada(xn_ref[...]), 0.0)
    n_seg = TM_F // SEG
    seg_rows = []
    for s in range(n_seg):
        seg = ht[s * SEG:(s + 1) * SEG]
        if tiles_per_seq is None:
            prev, nxt, base = zero_halo, zero_halo, 0
        else:
            prev = tile_prev if s == 0 else ht[s * SEG - HALO:s * SEG]
            nxt = tile_next if s == n_seg - 1 else ht[(s + 1) * SEG:(s + 1) * SEG + HALO]
            base = pos0 + s * SEG
        hs = jnp.concatenate([prev, seg, nxt], axis=0)
        hi = hs.astype(BF16)
        lo = (hs - hi.astype(F32)).astype(BF16)
        pos = base + lax.broadcasted_iota(jnp.int32, (SEG, POOL_GD), 0)
        parts = []
        for gi, w in enumerate(POOL_WINDOWS):
            half = w // 2
            cols = slice(gi * POOL_GD, (gi + 1) * POOL_GD)
            top = (jnp.dot(band_ref[gi, 0], hi[0:SEG, cols], preferred_element_type=F32)
                   + jnp.dot(band_ref[gi, 0], lo[0:SEG, cols], preferred_element_type=F32))
            bot = (jnp.dot(band_ref[gi, 1], hi[2 * HALO:, cols], preferred_element_type=F32)
                   + jnp.dot(band_ref[gi, 1], lo[2 * HALO:, cols], preferred_element_type=F32))
            tot = jnp.concatenate([top, bot], axis=0)
            cnt = (jnp.minimum(pos + half, seq_len) - jnp.maximum(pos - half, 0)).astype(F32)
            diff = (tot / cnt - seg[:, cols]).astype(BF16)
            parts.append(jnp.dot(diff, wp_ref[gi], preferred_element_type=F32))
        seg_rows.append(jnp.concatenate(parts, axis=-1))
    mix = jnp.concatenate(seg_rows, axis=0) * ps_ref[...]
    x1 = x + g1 * mix
    _mlp_tail(x1, mod_ref, n2g_ref, w1_ref, w2_ref, fg_ref, o_ref, final)


def _pool_bands():
    half_rows = SEG // 2
    t = np.arange(half_rows)[:, None]
    u = np.arange(SEG)[None, :]
    bands = np.zeros((len(POOL_WINDOWS), 2, half_rows, SEG), np.float32)
    for gi, w in enumerate(POOL_WINDOWS):
        half = w // 2
        d_top = (u - HALO) - t
        d_bot = (u + HALO) - (half_rows + t)
        bands[gi, 0] = (d_top >= -half) & (d_top <= half - 1)
        bands[gi, 1] = (d_bot >= -half) & (d_bot <= half - 1)
    return jnp.asarray(bands, BF16)


def _ffn_ab(x, attn, gm, mod_l, w, final, tiles_per_batch):
    n = x.shape[0]
    row = lambda i: (i, 0)
    w_arrs, w_specs = _weights(w, ["wo", "n2g", "w1", "w2", "fg"])
    return pl.pallas_call(
        functools.partial(_ffn_ab_kernel, final=final),
        grid=(n // TM_F,),
        in_specs=[
            pl.BlockSpec((TM_F, D), row),
            pl.BlockSpec((TM_F, HEADS * VDIM), row),
            pl.BlockSpec((TM_F, GM_W), row),
            _mod_spec(tiles_per_batch),
        ] + w_specs,
        out_specs=pl.BlockSpec((TM_F, D), row),
        out_shape=jax.ShapeDtypeStruct((n, D), F32),
        compiler_params=_params(1),
        name="ffn_ab",
    )(x, attn, gm, mod_l, *w_arrs)


def _ffn_pool(x, mod_l, w, final, tiles_per_batch):
    n = x.shape[0]
    row = lambda i: (i, 0)
    hb = TM_F // HALO
    w_arrs, w_specs = _weights(w, ["n1g", "bands", "wp", "ps", "n2g", "w1", "w2", "fg"])
    return pl.pallas_call(
        functools.partial(_ffn_pool_kernel, final=final, tiles_per_seq=tiles_per_batch),
        grid=(n // TM_F,),
        in_specs=[
            pl.BlockSpec((TM_F, D), row),
            pl.BlockSpec((HALO, D), lambda i: (jnp.maximum(i * hb - 1, 0), 0)),
            pl.BlockSpec((HALO, D), lambda i: (jnp.minimum((i + 1) * hb, n // HALO - 1), 0)),
            _mod_spec(tiles_per_batch),
        ] + w_specs,
        out_specs=pl.BlockSpec((TM_F, D), row),
        out_shape=jax.ShapeDtypeStruct((n, D), F32),
        compiler_params=_params(1),
        name="ffn_pool",
    )(x, x, x, mod_l, *w_arrs)


_SWAP = np.arange(ROPE).reshape(2, 2, ROPE // 4)[:, ::-1, :].reshape(ROPE)


def _rope_tables(t_len):
    rows = t_len // GRID_W
    row = jnp.repeat(jnp.arange(rows), GRID_W).astype(F32)
    col = jnp.tile(jnp.arange(GRID_W), rows).astype(F32)
    axis_dim = ROPE // 2
    inv = ROPE_BASE ** (-jnp.arange(0, axis_dim, 2, dtype=F32) / axis_dim)
    ang = jnp.stack([row[:, None] * inv, col[:, None] * inv], axis=1)
    cos = jnp.cos(ang)
    sin = jnp.sin(ang)
    c32 = jnp.stack([cos, cos], axis=2).reshape(t_len, ROPE)
    s32 = jnp.stack([-sin, sin], axis=2).reshape(t_len, ROPE)
    one = lambda r, c: jnp.ones((r, c), F32)
    zero = lambda r, c: jnp.zeros((r, c), F32)
    tq = jnp.concatenate([jnp.concatenate([one(t_len, NOPE), c32, s32], -1),
                          jnp.concatenate([one(TM, NOPE + ROPE), zero(TM, ROPE)], -1)], 0)
    ck = jnp.concatenate([jnp.concatenate([zero(t_len, NOPE), c32, c32], -1),
                          jnp.concatenate([zero(TM, NOPE), one(TM, 2 * ROPE)], -1)], 0)
    sk = jnp.concatenate([jnp.concatenate([zero(t_len, NOPE), s32, s32], -1), zero(TM, HEAD_PAD)], 0)
    return tq, ck, sk


def _prep_even_weights(w_in_ab, q_a_g, kv_a_g, w_q_b, w_kv_b, gmlp_v_g, w_spatial, b_spatial, w_out_ab):
    n_ab = w_in_ab.shape[0]
    kpe_off = Q_RANK + KV_RANK
    kp_w = w_in_ab[:, :, kpe_off:kpe_off + ROPE]
    kp_sw = kp_w[:, :, _SWAP]
    z = lambda c: jnp.zeros((n_ab, D, c), F32)
    wcat = jnp.concatenate(
        [w_in_ab[:, :, :kpe_off], w_in_ab[:, :, kpe_off + ROPE:],
         z(NOPE), kp_w, kp_w, z(NOPE), kp_sw, kp_sw, kp_w, z(HEAD_PAD - ROPE)], axis=2).astype(BF16)
    wq4 = w_q_b.reshape(n_ab, Q_RANK, HEADS, NOPE + ROPE)
    wq = jnp.concatenate([wq4, wq4[..., NOPE:][..., _SWAP]], axis=-1).reshape(n_ab, Q_RANK, QK_W).astype(BF16)
    wkv4 = w_kv_b.reshape(n_ab, KV_RANK, HEADS, NOPE + VDIM)
    wkb = jnp.pad(wkv4[..., :NOPE], ((0, 0), (0, 0), (0, 0), (0, HEAD_PAD - NOPE))).reshape(
        n_ab, KV_RANK, QK_W).astype(BF16)
    wvt = jnp.pad(wkv4[..., NOPE:].transpose(0, 2, 3, 1), ((0, 0), (0, 0), (0, V_ROWS - VDIM), (0, 0))).reshape(
        n_ab, VT_ROWS, KV_RANK).astype(BF16)
    ws = w_spatial.reshape(n_ab, GM_GROUPS // 2, 2, GM_CHUNK, GM_CHUNK).transpose(0, 1, 3, 2, 4).reshape(
        n_ab, GM_GROUPS // 2, GM_CHUNK, 2 * GM_CHUNK).astype(BF16)
    bs = jnp.repeat(b_spatial.transpose(0, 2, 1), GM_W // GM_GROUPS, axis=2)
    aug = np.zeros((VT_ROWS, 1), np.float32)
    aug[VDIM::V_ROWS] = 1.0
    return dict(wcat=wcat, wq=wq, wkb=wkb, wvt=wvt, ws=ws, bs=bs, aug=jnp.asarray(aug),
                qag=q_a_g[:, None], kvag=kv_a_g[:, None], vg=gmlp_v_g[:, None], wo=w_out_ab.astype(BF16))


def kernel(x_prompt, x_sample, cache_ckv, cache_kpe, c, c_ctx, w_mod, b_mod, norm1_g, norm2_g, w_in_ab, q_a_g,
           kv_a_g, w_q_b, w_kv_b, gmlp_v_g, w_spatial, b_spatial, w_out_ab, w_pool, pool_scale, w_ff1, w_ff2,
           final_g):
    batch, seq, _ = x_prompt.shape
    dec_batch, dec_seq, _ = x_sample.shape
    past = cache_ckv.shape[2]
    assert seq == CTX_SEQ_LEN and dec_seq % TM_F == 0 and dec_seq % TM == 0 and past % (2 * LANES) == 0
    lat_tiles = dec_seq // TM_F

    xc = x_prompt.reshape(batch * seq, D)
    xl = x_sample.reshape(dec_batch * dec_seq, D)
    cond8 = jnp.concatenate([c_ctx[None], c, jnp.zeros((MOD_ROWS - 1 - dec_batch, D), F32)], axis=0)
    mod = _modulation(cond8, w_mod, b_mod).reshape(DEPTH, MOD_ROWS, 1, N_MOD * D)
    tabs = _rope_tables(dec_seq)
    bands = _pool_bands()
    place = np.zeros((ROPE, HEAD_PAD), np.float32)
    place[np.arange(ROPE), NOPE + np.arange(ROPE)] = 1.0
    place[np.arange(ROPE), NOPE + ROPE + np.arange(ROPE)] = 1.0
    place = jnp.asarray(place, BF16)

    ev = _prep_even_weights(w_in_ab, q_a_g, kv_a_g, w_q_b, w_kv_b, gmlp_v_g, w_spatial, b_spatial, w_out_ab)
    w1 = w_ff1.astype(BF16)
    w2 = w_ff2.astype(BF16)
    wp = w_pool.astype(BF16)
    n1g = norm1_g[:, None]
    n2g = norm2_g[:, None]
    ps = pool_scale[:, None]

    new_ckv = new_kpe = None
    for l in range(DEPTH):
        final = l == DEPTH - 1
        i = l // 2
        w = dict(n1g=(n1g, l), n2g=(n2g, l), w1=(w1, l), w2=(w2, l), fg=final_g[None], bands=bands)
        if l % 2 == 0:
            w.update({name: (val, i) for name, val in ev.items() if name != "aug"}, aug=ev["aug"])
            qc, kc, vtc, gmc, kn2_ctx, new_ckv, new_kpe = _in_proj(
                xc, mod[l], w, tabs, layer=i, ckv_prev=new_ckv, kpe_prev=new_kpe)
            ql, kl, vtl, gml, kn2_l = _in_proj(xl, mod[l], w, tabs, layer=i, lat_shape=(dec_batch, dec_seq))
            k_cache, vt_cache, kn2_c = _cache_keys(cache_ckv, cache_kpe, w, place, layer=i)
            kmax2 = jnp.maximum(kn2_l.reshape(dec_batch, -1, HEADS, LANES).max(axis=1), kn2_c)
            attn_c = _attention(qc, [(kc, vtc)], kn2_ctx, batch=batch * seq // TM, t_len=TM, segment=seq)
            attn_l = _attention(ql, [(k_cache, vt_cache), (kl.reshape(dec_batch, dec_seq, QK_W), vtl)], kmax2,
                                batch=dec_batch, t_len=dec_seq)
            xc = _ffn_ab(xc, attn_c, gmc, mod[l], w, final, None)
            xl = _ffn_ab(xl, attn_l, gml, mod[l], w, final, lat_tiles)
        else:
            w.update(wp=(wp, i), ps=(ps, i))
            xc = _ffn_pool(xc, mod[l], w, final, None)
            xl = _ffn_pool(xl, mod[l], w, final, lat_tiles)

    return xc.reshape(batch, seq, D), xl.reshape(dec_batch, dec_seq, D), new_ckv, new_kpe
```

```python
import functools
import math

import numpy as np
import jax
import jax.numpy as jnp
from jax import lax
from jax.experimental import pallas as pl
from jax.experimental.pallas import tpu as pltpu

F32 = jnp.float32
BF16 = jnp.bfloat16

D = 1024
DEPTH = 4
N_MOD = 6
HEADS = 8
NOPE = 64
ROPE = 32
VDIM = 64
Q_RANK = 384
KV_RANK = 256
GRID_W = 64
ROPE_BASE = 10000.0
ATTN_SCALE = (NOPE + ROPE) ** -0.5
GM_W = 512
GM_GROUPS = 8
GM_CHUNK = 128
POOL_WINDOWS = (2, 4, 8, 16)
POOL_GD = D // len(POOL_WINDOWS)
D_FF = 4 * D
EPS = 1e-6
CTX_SEQ_LEN = 256

LANES = 128
HEAD_PAD = LANES
QK_W = HEADS * HEAD_PAD
V_ROWS = VDIM + 16
VT_ROWS = HEADS * V_ROWS
MOD_ROWS = 8
HALO = 8
SEG = 256

C_QA = 0
C_KVA = C_QA + Q_RANK
C_U = C_KVA + KV_RANK
C_V = C_U + GM_W
C_KP = C_V + GM_W
C_KPS = C_KP + HEAD_PAD
C_KO = C_KPS + HEAD_PAD
IN_W = C_KO + HEAD_PAD

TM = 1024
TM_F = 1024
TF = 2048
TQ = 256
Q_TILES = 4
MAX_KEY_BLK = 2048
VMEM_LIMIT = 56 * 1024 * 1024


def _rms(x, g):
    return x * lax.rsqrt(jnp.mean(x * x, axis=-1, keepdims=True) + EPS) * g


def _const_spec(shape):
    nd = len(shape)
    return pl.BlockSpec(shape, lambda *_: (0,) * nd)


def _weights(w, names):
    arrs, specs = [], []
    for name in names:
        entry = w[name]
        if isinstance(entry, tuple):
            arr, layer = entry
            shape = arr.shape[1:]
            specs.append(pl.BlockSpec((None,) + shape, lambda *_, layer=layer, nd=len(shape): (layer,) + (0,) * nd,
                                      pipeline_mode=pl.Buffered(1)))
        else:
            arr = entry
            specs.append(pl.BlockSpec(arr.shape, lambda *_, nd=arr.ndim: (0,) * nd, pipeline_mode=pl.Buffered(1)))
        arrs.append(arr)
    return arrs, specs


def _params(n_axes, flags=None):
    return pltpu.CompilerParams(dimension_semantics=("arbitrary",) * n_axes, vmem_limit_bytes=VMEM_LIMIT,
                                flags=flags)


def _mod_kernel(cond_ref, w_ref, b_ref, o_ref):
    c = cond_ref[...]
    s = (c / (1.0 + jnp.exp(-c))).astype(BF16)
    w = w_ref[...].astype(BF16)
    o_ref[...] = jnp.dot(s, w, preferred_element_type=F32) + b_ref[...]


def _modulation(cond8, w_mod, b_mod):
    tn = 1536
    nw = N_MOD * D
    return pl.pallas_call(
        _mod_kernel,
        grid=(DEPTH, nw // tn),
        in_specs=[
            pl.BlockSpec((MOD_ROWS, D), lambda l, j: (0, 0)),
            pl.BlockSpec((None, D, tn), lambda l, j: (l, 0, j)),
            pl.BlockSpec((None, 1, tn), lambda l, j: (l, 0, j)),
        ],
        out_specs=pl.BlockSpec((None, MOD_ROWS, tn), lambda l, j: (l, 0, j)),
        out_shape=jax.ShapeDtypeStruct((DEPTH, MOD_ROWS, nw), F32),
        compiler_params=_params(2),
        name="modulation",
    )(cond8, w_mod, b_mod.reshape(DEPTH, 1, nw))


def _mod_spec(tiles_per_batch):
    if tiles_per_batch is None:
        return pl.BlockSpec((None, 1, N_MOD * D), lambda i: (0, 0, 0))
    return pl.BlockSpec((None, 1, N_MOD * D), lambda i: (1 + i // tiles_per_batch, 0, 0))


def _max_key_norm2(keys):
    kf = keys.astype(F32)
    ksq = kf * kf
    rows = []
    for h in range(HEADS):
        norm2 = jnp.sum(ksq[:, h * HEAD_PAD:(h + 1) * HEAD_PAD], axis=-1, keepdims=True)
        rows.append(jnp.broadcast_to(jnp.max(norm2, axis=0, keepdims=True), (1, LANES)))
    return jnp.concatenate(rows, axis=0)


def _in_kernel(x_ref, mod_ref, n1g_ref, wcat_ref, qag_ref, kvag_ref, wq_ref, wkb_ref, wvt_ref, aug_ref,
               vg_ref, ws_ref, bs_ref, tq_ref, ck_ref, sk_ref, *rest, is_ctx, layer, n_prev):
    prev_refs, out_refs = rest[:n_prev], rest[n_prev:]
    q_ref, k_ref, vt_ref, gm_ref, kn2_ref = out_refs[:5]
    if is_ctx:
        ckv_ref, kpe_ref = out_refs[5:]
    tm = x_ref.shape[0]
    x = x_ref[...]
    sh1 = mod_ref[:, 0 * D:1 * D]
    sc1 = mod_ref[:, 1 * D:2 * D]
    h = (_rms(x, n1g_ref[...]) * (1.0 + sc1) + sh1).astype(BF16)
    z = jnp.dot(h, wcat_ref[...], preferred_element_type=F32)

    qn = _rms(z[:, C_QA:C_QA + Q_RANK], qag_ref[...]).astype(BF16)
    qa = jnp.dot(qn, wq_ref[...], preferred_element_type=F32)
    tq8 = jnp.concatenate([tq_ref[...]] * HEADS, axis=-1)
    q_ref[...] = (qa * tq8 * (ATTN_SCALE * math.log2(math.e))).astype(BF16)

    ckv = _rms(z[:, C_KVA:C_KVA + KV_RANK], kvag_ref[...])
    kp_rot = z[:, C_KP:C_KP + HEAD_PAD] * ck_ref[...] + z[:, C_KPS:C_KPS + HEAD_PAD] * sk_ref[...]
    ckv_b = ckv.astype(BF16)
    kn = jnp.dot(ckv_b, wkb_ref[...], preferred_element_type=F32)
    keys = (kn + jnp.concatenate([kp_rot] * HEADS, axis=-1)).astype(BF16)
    k_ref[...] = keys
    kn2_ref[...] = _max_key_norm2(keys)
    vt = lax.dot_general(wvt_ref[...], ckv_b, (((1,), (1,)), ((), ())), preferred_element_type=F32)
    vt_ref[...] = (vt + aug_ref[...]).astype(BF16)
    if is_ctx:
        seqs, n_ab = ckv_ref.shape[0], ckv_ref.shape[1]
        for slot in range(n_ab):
            if slot == layer:
                ckv_ref[:, slot] = ckv.reshape(seqs, CTX_SEQ_LEN, KV_RANK)
                kpe_ref[:, slot] = z[:, C_KO:C_KO + ROPE].reshape(seqs, CTX_SEQ_LEN, ROPE)
            elif prev_refs:
                ckv_ref[:, slot] = prev_refs[0][:, slot]
                kpe_ref[:, slot] = prev_refs[1][:, slot]
            else:
                ckv_ref[:, slot] = jnp.zeros((seqs, CTX_SEQ_LEN, KV_RANK), F32)
                kpe_ref[:, slot] = jnp.zeros((seqs, CTX_SEQ_LEN, ROPE), F32)

    u = jax.nn.gelu(z[:, C_U:C_U + GM_W])
    vv = jax.nn.gelu(z[:, C_V:C_V + GM_W])
    vn = _rms(vv, vg_ref[...])
    lane = lax.broadcasted_iota(jnp.int32, (GM_CHUNK, LANES), 1)
    lo = lane < GM_W // GM_GROUPS
    for c in range(tm // GM_CHUNK):
        rows = slice(c * GM_CHUNK, (c + 1) * GM_CHUNK)
        for j in range(GM_W // LANES):
            cols = slice(j * LANES, (j + 1) * LANES)
            blk = vn[rows, cols]
            rhs = jnp.concatenate([jnp.where(lo, blk, 0.0), jnp.where(lo, 0.0, blk)], axis=0).astype(BF16)
            s = jnp.dot(ws_ref[j], rhs, preferred_element_type=F32) + bs_ref[:, cols]
            gm_ref[rows, cols] = (u[rows, cols] * s).astype(BF16)


def _in_proj(x, mod_l, w, tabs, *, layer, ckv_prev=None, kpe_prev=None, lat_shape=None):
    n = x.shape[0]
    is_ctx = lat_shape is None
    row = lambda i: (i, 0)
    id_blk = tabs[0].shape[0] // TM - 1
    if is_ctx:
        tpb = None
        tab_idx = lambda i: (id_blk, 0)
        vt_shape, vt_spec = (VT_ROWS, n), pl.BlockSpec((VT_ROWS, TM), lambda i: (0, i))
    else:
        dec_batch, dec_seq = lat_shape
        tpb = dec_seq // TM
        tab_idx = lambda i: (i % tpb, 0)
        vt_shape = (dec_batch, VT_ROWS, dec_seq)
        vt_spec = pl.BlockSpec((None, VT_ROWS, TM), lambda i: (i // tpb, 0, i % tpb))
    w_arrs, w_specs = _weights(w, ["n1g", "wcat", "qag", "kvag", "wq", "wkb", "wvt", "aug", "vg", "ws", "bs"])
    in_specs = [pl.BlockSpec((TM, D), row), _mod_spec(tpb)] + w_specs + [pl.BlockSpec((TM, HEAD_PAD), tab_idx)] * 3
    args = [x, mod_l, *w_arrs, *tabs]
    out_shape = [jax.ShapeDtypeStruct((n, QK_W), BF16), jax.ShapeDtypeStruct((n, QK_W), BF16),
                 jax.ShapeDtypeStruct(vt_shape, BF16), jax.ShapeDtypeStruct((n, GM_W), BF16),
                 jax.ShapeDtypeStruct((n // TM, HEADS, LANES), F32)]
    out_specs = [pl.BlockSpec((TM, QK_W), row), pl.BlockSpec((TM, QK_W), row), vt_spec,
                 pl.BlockSpec((TM, GM_W), row), pl.BlockSpec((None, HEADS, LANES), lambda i: (i, 0, 0))]
    aliases = {}
    n_prev = 0
    if is_ctx:
        seqs = TM // CTX_SEQ_LEN
        batch = n // CTX_SEQ_LEN
        n_ab = DEPTH // 2
        cache_specs = [pl.BlockSpec((seqs, n_ab, CTX_SEQ_LEN, KV_RANK), lambda i: (i, 0, 0, 0)),
                       pl.BlockSpec((seqs, n_ab, CTX_SEQ_LEN, ROPE), lambda i: (i, 0, 0, 0))]
        out_shape += [jax.ShapeDtypeStruct((batch, n_ab, CTX_SEQ_LEN, KV_RANK), F32),
                      jax.ShapeDtypeStruct((batch, n_ab, CTX_SEQ_LEN, ROPE), F32)]
        out_specs += cache_specs
        if ckv_prev is not None:
            in_specs += cache_specs
            args += [ckv_prev, kpe_prev]
            aliases = {len(args) - 2: 5, len(args) - 1: 6}
            n_prev = 2

    return pl.pallas_call(
        functools.partial(_in_kernel, is_ctx=is_ctx, layer=layer, n_prev=n_prev),
        grid=(n // TM,),
        in_specs=in_specs,
        out_specs=out_specs,
        out_shape=out_shape,
        input_output_aliases=aliases,
        compiler_params=_params(1),
        name="in_proj_ctx" if is_ctx else "in_proj_lat",
    )(*args)


def _cache_kernel(ckv_ref, kpe_ref, wkb_ref, wvt_ref, aug_ref, place_ref, k_ref, vt_ref, kn2_ref):
    ckv_b = ckv_ref[...].astype(BF16)
    kn = jnp.dot(ckv_b, wkb_ref[...], preferred_element_type=F32)
    kp = jnp.dot(kpe_ref[...].astype(BF16), place_ref[...], preferred_element_type=F32)
    keys = (kn + jnp.concatenate([kp] * HEADS, axis=-1)).astype(BF16)
    k_ref[...] = keys
    kn2_ref[...] = _max_key_norm2(keys)
    vt = lax.dot_general(wvt_ref[...], ckv_b, (((1,), (1,)), ((), ())), preferred_element_type=F32)
    vt_ref[...] = (vt + aug_ref[...]).astype(BF16)


def _cache_keys(cache_ckv, cache_kpe, w, place, *, layer):
    dec_batch, _, past, _ = cache_ckv.shape
    w_arrs, w_specs = _weights(dict(w, place=place), ["wkb", "wvt", "aug", "place"])
    return pl.pallas_call(
        _cache_kernel,
        grid=(dec_batch,),
        in_specs=[
            pl.BlockSpec((None, None, past, KV_RANK), lambda b: (b, layer, 0, 0)),
            pl.BlockSpec((None, None, past, ROPE), lambda b: (b, layer, 0, 0)),
            *w_specs,
        ],
        out_specs=[pl.BlockSpec((None, past, QK_W), lambda b: (b, 0, 0)),
                   pl.BlockSpec((None, VT_ROWS, past), lambda b: (b, 0, 0)),
                   pl.BlockSpec((None, HEADS, LANES), lambda b: (b, 0, 0))],
        out_shape=(jax.ShapeDtypeStruct((dec_batch, past, QK_W), BF16),
                   jax.ShapeDtypeStruct((dec_batch, VT_ROWS, past), BF16),
                   jax.ShapeDtypeStruct((dec_batch, HEADS, LANES), F32)),
        compiler_params=_params(1),
        name="cache_keys",
    )(cache_ckv, cache_kpe, *w_arrs)


SAFE_DENOM = 2.0 ** -100
BOUND_SLACK = 1.02


def _attn_kernel(q_ref, *refs, n_src, segment):
    srcs = [(refs[2 * s], refs[2 * s + 1]) for s in range(n_src)]
    kmax_ref, o_ref, st_ref, p_ref = refs[2 * n_src:]

    def key_blocks(qt):
        blocks = []
        base = 0
        for k_ref, vt_ref in srcs:
            l_src = k_ref.shape[0] if segment is None else segment
            for start in range(0, l_src, MAX_KEY_BLK):
                size = min(MAX_KEY_BLK, l_src - start)
                if segment is None:
                    src_rows = slice(start, start + size)
                elif isinstance(qt, int):
                    src_rows = slice(qt * segment + start, qt * segment + start + size)
                else:
                    src_rows = pl.ds(pl.multiple_of(qt * segment + start, 2 * LANES), size)
                blocks.append((k_ref, vt_ref, src_rows, slice(base + start, base + start + size)))
            base += l_src
        return blocks

    n_qt = q_ref.shape[0] // TQ
    items = [(qt, h) for qt in range(n_qt) for h in range(HEADS)]
    n_items = len(items)
    ones = jnp.ones((8, HEAD_PAD), BF16)

    def finish(ot, qt, h, pair):
        pair[h] = ot[0:VDIM] * (1.0 / ot[VDIM:VDIM + 1])
        if h % 2 == 1:
            both = jnp.concatenate([pair.pop(h - 1), pair.pop(h)], axis=0)
            o_ref[qt * TQ:(qt + 1) * TQ, (h // 2) * LANES:(h // 2 + 1) * LANES] = both.T.astype(BF16)

    pair = {}
    bad = None
    for step in range(n_items + 1):
        ia, ic = step, step - 1
        ot = None
        if ia < n_items:
            qt, h = items[ia]
            cols = slice(h * HEAD_PAD, (h + 1) * HEAD_PAD)
            qh = q_ref[qt * TQ:(qt + 1) * TQ, cols]
            qf = qh.astype(F32)
            qn2 = lax.dot_general(ones, (qf * qf).astype(BF16), (((1,), (1,)), ((), ())),
                                  preferred_element_type=F32)[0:1]
            km2 = kmax_ref[h:h + 1, :]
            shift = jnp.sqrt(qn2 * jnp.concatenate([km2] * (TQ // LANES), axis=-1)) * BOUND_SLACK
        for j in range(len(key_blocks(0))):
            if ia < n_items:
                k_ref, _, src_rows, rows = key_blocks(qt)[j]
                st = lax.dot_general(k_ref[src_rows, cols], qh, (((1,), (1,)), ((), ())),
                                     preferred_element_type=F32)
                p_ref[ia % 2, rows, :] = jnp.exp2(st - shift).astype(BF16)
            if 0 <= ic < n_items:
                qc, hc = items[ic]
                _, vt_ref, src_rows, rows = key_blocks(qc)[j]
                part = jnp.dot(vt_ref[hc * V_ROWS:(hc + 1) * V_ROWS, src_rows], p_ref[ic % 2, rows, :],
                               preferred_element_type=F32)
                ot = part if ot is None else ot + part
        if 0 <= ic < n_items:
            unsafe = jnp.where(ot[VDIM:VDIM + 1] >= SAFE_DENOM, 0.0, 1.0)
            bad = unsafe if bad is None else jnp.maximum(bad, unsafe)
            finish(ot, *items[ic], pair)

    @pl.when(jnp.max(bad) > 0.0)
    def _():
        def redo(i, carry):
            qt = i // (HEADS // 2)
            hp = i % (HEADS // 2)
            q_rows = pl.ds(pl.multiple_of(qt * TQ, TQ), TQ)
            blocks = key_blocks(qt)
            outs = []
            for e in range(2):
                h = 2 * hp + e
                cols = pl.ds(pl.multiple_of(h * HEAD_PAD, HEAD_PAD), HEAD_PAD)
                qh = q_ref[q_rows, cols]
                for k_ref, _, src_rows, rows in blocks:
                    st_ref[0, rows, :] = lax.dot_general(k_ref[src_rows, cols], qh, (((1,), (1,)), ((), ())),
                                                         preferred_element_type=F32)
                m = jnp.max(st_ref[0], axis=0, keepdims=True)
                p_ref[0] = jnp.exp2(st_ref[0] - m).astype(BF16)
                ot = None
                v_rows = pl.ds(pl.multiple_of(h * V_ROWS, 16), V_ROWS)
                for _, vt_ref, src_rows, rows in blocks:
                    part = jnp.dot(vt_ref[v_rows, src_rows], p_ref[0, rows, :], preferred_element_type=F32)
                    ot = part if ot is None else ot + part
                outs.append(ot[0:VDIM] * (1.0 / ot[VDIM:VDIM + 1]))
            o_ref[q_rows, pl.ds(pl.multiple_of(hp * LANES, LANES), LANES)] = (
                jnp.concatenate(outs, axis=0).T.astype(BF16))
            return carry

        lax.fori_loop(0, n_qt * (HEADS // 2), redo, 0)


def _attention(q, sources, kmax2, *, batch, t_len, segment=None):
    tq = min(t_len, Q_TILES * TQ)
    nq = t_len // tq
    in_specs = [pl.BlockSpec((tq, QK_W), lambda b, qi: (b * nq + qi, 0))]
    args = [q]
    l_total = 0
    for k, vt in sources:
        if segment is None:
            l_len = k.shape[1]
            l_total += l_len
            in_specs += [pl.BlockSpec((None, l_len, QK_W), lambda b, qi: (b, 0, 0)),
                         pl.BlockSpec((None, VT_ROWS, l_len), lambda b, qi: (b, 0, 0))]
        else:
            assert segment == TQ and nq == 1
            l_total += segment
            in_specs += [pl.BlockSpec((tq, QK_W), lambda b, qi: (b, 0)),
                         pl.BlockSpec((VT_ROWS, tq), lambda b, qi: (0, b))]
        args += [k, vt]
    in_specs.append(pl.BlockSpec((None, HEADS, LANES), lambda b, qi: (b, 0, 0)))
    args.append(kmax2)
    return pl.pallas_call(
        functools.partial(_attn_kernel, n_src=len(sources), segment=segment),
        grid=(batch, nq),
        scratch_shapes=[pltpu.VMEM((1, l_total, TQ), F32), pltpu.VMEM((2, l_total, TQ), BF16)],
        in_specs=in_specs,
        out_specs=pl.BlockSpec((tq, HEADS * VDIM), lambda b, qi: (b * nq + qi, 0)),
        out_shape=jax.ShapeDtypeStruct((batch * t_len, HEADS * VDIM), BF16),
        compiler_params=_params(2),
        name="attention",
    )(*args)


def _mlp_tail(x1, mod_ref, n2g_ref, w1_ref, w2_ref, fg_ref, o_ref, final):
    sh2 = mod_ref[:, 3 * D:4 * D]
    sc2 = mod_ref[:, 4 * D:5 * D]
    g2 = mod_ref[:, 5 * D:6 * D]
    h = (_rms(x1, n2g_ref[...]) * (1.0 + sc2) + sh2).astype(BF16)
    acc = None
    for c in range(D_FF // TF):
        a = jnp.maximum(jnp.dot(h, w1_ref[:, c * TF:(c + 1) * TF], preferred_element_type=F32), 0.0)
        part = jnp.dot((a * a).astype(BF16), w2_ref[c * TF:(c + 1) * TF, :], preferred_element_type=F32)
        acc = part if acc is None else acc + part
    x2 = x1 + g2 * acc
    if final:
        x2 = _rms(x2, fg_ref[...])
    o_ref[...] = x2


def _ffn_ab_kernel(x_ref, attn_ref, gm_ref, mod_ref, wo_ref, n2g_ref, w1_ref, w2_ref, fg_ref, o_ref, *, final):
    g1 = mod_ref[:, 2 * D:3 * D]
    mix = (jnp.dot(attn_ref[...], wo_ref[0:HEADS * VDIM, :], preferred_element_type=F32)
           + jnp.dot(gm_ref[...], wo_ref[HEADS * VDIM:, :], preferred_element_type=F32))
    x1 = x_ref[...] + g1 * mix
    _mlp_tail(x1, mod_ref, n2g_ref, w1_ref, w2_ref, fg_ref, o_ref, final)


def _ffn_pool_kernel(x_ref, xp_ref, xn_ref, mod_ref, n1g_ref, band_ref, wp_ref, ps_ref, n2g_ref, w1_ref, w2_ref,
                     fg_ref, o_ref, *, final, tiles_per_seq):
    sh1 = mod_ref[:, 0 * D:1 * D]
    sc1 = mod_ref[:, 1 * D:2 * D]
    g1 = mod_ref[:, 2 * D:3 * D]
    n1g = n1g_ref[...]
    ada = lambda v: _rms(v, n1g) * (1.0 + sc1) + sh1
    x = x_ref[...]
    ht = ada(x)
    zero_halo = jnp.zeros((HALO, D), F32)
    if tiles_per_seq is None:
        seq_len, pos0 = SEG, 0
    else:
        t = pl.program_id(0) % tiles_per_seq
        seq_len, pos0 = tiles_per_seq * TM_F, t * TM_F
        tile_prev = jnp.where(t != 0, ada(xp_ref[...]), 0.0)
        tile_next = jnp.where(t != tiles_per_seq - 1, ada(xn_ref[...]), 0.0)
    n_seg = TM_F // SEG
    seg_rows = []
    for s in range(n_seg):
        seg = ht[s * SEG:(s + 1) * SEG]
        if tiles_per_seq is None:
            prev, nxt, base = zero_halo, zero_halo, 0
        else:
            prev = tile_prev if s == 0 else ht[s * SEG - HALO:s * SEG]
            nxt = tile_next if s == n_seg - 1 else ht[(s + 1) * SEG:(s + 1) * SEG + HALO]
            base = pos0 + s * SEG
        hs = jnp.concatenate([prev, seg, nxt], axis=0)
        hi = hs.astype(BF16)
        lo = (hs - hi.astype(F32)).astype(BF16)
        pos = base + lax.broadcasted_iota(jnp.int32, (SEG, POOL_GD), 0)
        parts = []
        for gi, w in enumerate(POOL_WINDOWS):
            half = w // 2
            cols = slice(gi * POOL_GD, (gi + 1) * POOL_GD)
            top = (jnp.dot(band_ref[gi, 0], hi[0:SEG, cols], preferred_element_type=F32)
                   + jnp.dot(band_ref[gi, 0], lo[0:SEG, cols], preferred_element_type=F32))
            bot = (jnp.dot(band_ref[gi, 1], hi[2 * HALO:, cols], preferred_element_type=F32)
                   + jnp.dot(band_ref[gi, 1], lo[2 * HALO:, cols], preferred_element_type=F32))
            tot = jnp.concatenate([top, bot], axis=0)
            cnt = (jnp.minimum(pos + half, seq_len) - jnp.maximum(pos - half, 0)).astype(F32)
            diff = (tot / cnt - seg[:, cols]).astype(BF16)
            parts.append(jnp.dot(diff, wp_ref[gi], preferred_element_type=F32))
        seg_rows.append(jnp.concatenate(parts, axis=-1))
    mix = jnp.concatenate(seg_rows, axis=0) * ps_ref[...]
    x1 = x + g1 * mix
    _mlp_tail(x1, mod_ref, n2g_ref, w1_ref, w2_ref, fg_ref, o_ref, final)


def _pool_bands():
    half_rows = SEG // 2
    t = np.arange(half_rows)[:, None]
    u = np.arange(SEG)[None, :]
    bands = np.zeros((len(POOL_WINDOWS), 2, half_rows, SEG), np.float32)
    for gi, w in enumerate(POOL_WINDOWS):
        half = w // 2
        d_top = (u - HALO) - t
        d_bot = (u + HALO) - (half_rows + t)
        bands[gi, 0] = (d_top >= -half) & (d_top <= half - 1)
        bands[gi, 1] = (d_bot >= -half) & (d_bot <= half - 1)
    return jnp.asarray(bands, BF16)


def _ffn_ab(x, attn, gm, mod_l, w, final, tiles_per_batch):
    n = x.shape[0]
    row = lambda i: (i, 0)
    w_arrs, w_specs = _weights(w, ["wo", "n2g", "w1", "w2", "fg"])
    return pl.pallas_call(
        functools.partial(_ffn_ab_kernel, final=final),
        grid=(n // TM_F,),
        in_specs=[
            pl.BlockSpec((TM_F, D), row),
            pl.BlockSpec((TM_F, HEADS * VDIM), row),
            pl.BlockSpec((TM_F, GM_W), row),
            _mod_spec(tiles_per_batch),
        ] + w_specs,
        out_specs=pl.BlockSpec((TM_F, D), row),
        out_shape=jax.ShapeDtypeStruct((n, D), F32),
        compiler_params=_params(1),
        name="ffn_ab",
    )(x, attn, gm, mod_l, *w_arrs)


def _ffn_pool(x, mod_l, w, final, tiles_per_batch):
    n = x.shape[0]
    row = lambda i: (i, 0)
    hb = TM_F // HALO
    w_arrs, w_specs = _weights(w, ["n1g", "bands", "wp", "ps", "n2g", "w1", "w2", "fg"])
    return pl.pallas_call(
        functools.partial(_ffn_pool_kernel, final=final, tiles_per_seq=tiles_per_batch),
        grid=(n // TM_F,),
        in_specs=[
            pl.BlockSpec((TM_F, D), row),
            pl.BlockSpec((HALO, D), lambda i: (jnp.maximum(i * hb - 1, 0), 0)),
            pl.BlockSpec((HALO, D), lambda i: (jnp.minimum((i + 1) * hb, n // HALO - 1), 0)),
            _mod_spec(tiles_per_batch),
        ] + w_specs,
        out_specs=pl.BlockSpec((TM_F, D), row),
        out_shape=jax.ShapeDtypeStruct((n, D), F32),
        compiler_params=_params(1),
        name="ffn_pool",
    )(x, x, x, mod_l, *w_arrs)


_SWAP = np.arange(ROPE).reshape(2, 2, ROPE // 4)[:, ::-1, :].reshape(ROPE)


def _rope_tables(t_len):
    rows = t_len // GRID_W
    row = jnp.repeat(jnp.arange(rows), GRID_W).astype(F32)
    col = jnp.tile(jnp.arange(GRID_W), rows).astype(F32)
    axis_dim = ROPE // 2
    inv = ROPE_BASE ** (-jnp.arange(0, axis_dim, 2, dtype=F32) / axis_dim)
    ang = jnp.stack([row[:, None] * inv, col[:, None] * inv], axis=1)
    cos = jnp.cos(ang)
    sin = jnp.sin(ang)
    c32 = jnp.stack([cos, cos], axis=2).reshape(t_len, ROPE)
    s32 = jnp.stack([-sin, sin], axis=2).reshape(t_len, ROPE)
    one = lambda r, c: jnp.ones((r, c), F32)
    zero = lambda r, c: jnp.zeros((r, c), F32)
    tq = jnp.concatenate([jnp.concatenate([one(t_len, NOPE), c32, s32], -1),
                          jnp.concatenate([one(TM, NOPE + ROPE), zero(TM, ROPE)], -1)], 0)
    ck = jnp.concatenate([jnp.concatenate([zero(t_len, NOPE), c32, c32], -1),
                          jnp.concatenate([zero(TM, NOPE), one(TM, 2 * ROPE)], -1)], 0)
    sk = jnp.concatenate([jnp.concatenate([zero(t_len, NOPE), s32, s32], -1), zero(TM, HEAD_PAD)], 0)
    return tq, ck, sk


def _prep_even_weights(w_in_ab, q_a_g, kv_a_g, w_q_b, w_kv_b, gmlp_v_g, w_spatial, b_spatial, w_out_ab):
    n_ab = w_in_ab.shape[0]
    kpe_off = Q_RANK + KV_RANK
    kp_w = w_in_ab[:, :, kpe_off:kpe_off + ROPE]
    kp_sw = kp_w[:, :, _SWAP]
    z = lambda c: jnp.zeros((n_ab, D, c), F32)
    wcat = jnp.concatenate(
        [w_in_ab[:, :, :kpe_off], w_in_ab[:, :, kpe_off + ROPE:],
         z(NOPE), kp_w, kp_w, z(NOPE), kp_sw, kp_sw, kp_w, z(HEAD_PAD - ROPE)], axis=2).astype(BF16)
    wq4 = w_q_b.reshape(n_ab, Q_RANK, HEADS, NOPE + ROPE)
    wq = jnp.concatenate([wq4, wq4[..., NOPE:][..., _SWAP]], axis=-1).reshape(n_ab, Q_RANK, QK_W).astype(BF16)
    wkv4 = w_kv_b.reshape(n_ab, KV_RANK, HEADS, NOPE + VDIM)
    wkb = jnp.pad(wkv4[..., :NOPE], ((0, 0), (0, 0), (0, 0), (0, HEAD_PAD - NOPE))).reshape(
        n_ab, KV_RANK, QK_W).astype(BF16)
    wvt = jnp.pad(wkv4[..., NOPE:].transpose(0, 2, 3, 1), ((0, 0), (0, 0), (0, V_ROWS - VDIM), (0, 0))).reshape(
        n_ab, VT_ROWS, KV_RANK).astype(BF16)
    ws = w_spatial.reshape(n_ab, GM_GROUPS // 2, 2, GM_CHUNK, GM_CHUNK).transpose(0, 1, 3, 2, 4).reshape(
        n_ab, GM_GROUPS // 2, GM_CHUNK, 2 * GM_CHUNK).astype(BF16)
    bs = jnp.repeat(b_spatial.transpose(0, 2, 1), GM_W // GM_GROUPS, axis=2)
    aug = np.zeros((VT_ROWS, 1), np.float32)
    aug[VDIM::V_ROWS] = 1.0
    return dict(wcat=wcat, wq=wq, wkb=wkb, wvt=wvt, ws=ws, bs=bs, aug=jnp.asarray(aug),
                qag=q_a_g[:, None], kvag=kv_a_g[:, None], vg=gmlp_v_g[:, None], wo=w_out_ab.astype(BF16))


def kernel(x_prompt, x_sample, cache_ckv, cache_kpe, c, c_ctx, w_mod, b_mod, norm1_g, norm2_g, w_in_ab, q_a_g,
           kv_a_g, w_q_b, w_kv_b, gmlp_v_g, w_spatial, b_spatial, w_out_ab, w_pool, pool_scale, w_ff1, w_ff2,
           final_g):
    batch, seq, _ = x_prompt.shape
    dec_batch, dec_seq, _ = x_sample.shape
    past = cache_ckv.shape[2]
    assert seq == CTX_SEQ_LEN and dec_seq % TM_F == 0 and dec_seq % TM == 0 and past % (2 * LANES) == 0
    lat_tiles = dec_seq // TM_F

    xc = x_prompt.reshape(batch * seq, D)
    xl = x_sample.reshape(dec_batch * dec_seq, D)
    cond8 = jnp.concatenate([c_ctx[None], c, jnp.zeros((MOD_ROWS - 1 - dec_batch, D), F32)], axis=0)
    mod = _modulation(cond8, w_mod, b_mod).reshape(DEPTH, MOD_ROWS, 1, N_MOD * D)
    tabs = _rope_tables(dec_seq)
    bands = _pool_bands()
    place = np.zeros((ROPE, HEAD_PAD), np.float32)
    place[np.arange(ROPE), NOPE + np.arange(ROPE)] = 1.0
    place[np.arange(ROPE), NOPE + ROPE + np.arange(ROPE)] = 1.0
    place = jnp.asarray(place, BF16)

    ev = _prep_even_weights(w_in_ab, q_a_g, kv_a_g, w_q_b, w_kv_b, gmlp_v_g, w_spatial, b_spatial, w_out_ab)
    w1 = w_ff1.astype(BF16)
    w2 = w_ff2.astype(BF16)
    wp = w_pool.astype(BF16)
    n1g = norm1_g[:, None]
    n2g = norm2_g[:, None]
    ps = pool_scale[:, None]

    new_ckv = new_kpe = None
    for l in range(DEPTH):
        final = l == DEPTH - 1
        i = l // 2
        w = dict(n1g=(n1g, l), n2g=(n2g, l), w1=(w1, l), w2=(w2, l), fg=final_g[None], bands=bands)
        if l % 2 == 0:
            w.update({name: (val, i) for name, val in ev.items() if name != "aug"}, aug=ev["aug"])
            qc, kc, vtc, gmc, kn2_ctx, new_ckv, new_kpe = _in_proj(
                xc, mod[l], w, tabs, layer=i, ckv_prev=new_ckv, kpe_prev=new_kpe)
            ql, kl, vtl, gml, kn2_l = _in_proj(xl, mod[l], w, tabs, layer=i, lat_shape=(dec_batch, dec_seq))
            k_cache, vt_cache, kn2_c = _cache_keys(cache_ckv, cache_kpe, w, place, layer=i)
            kmax2 = jnp.maximum(kn2_l.reshape(dec_batch, -1, HEADS, LANES).max(axis=1), kn2_c)
            attn_c = _attention(qc, [(kc, vtc)], kn2_ctx, batch=batch * seq // TM, t_len=TM, segment=seq)
            attn_l = _attention(ql, [(k_cache, vt_cache), (kl.reshape(dec_batch, dec_seq, QK_W), vtl)], kmax2,
                                batch=dec_batch, t_len=dec_seq)
            xc = _ffn_ab(xc, attn_c, gmc, mod[l], w, final, None)
            xl = _ffn_ab(xl, attn_l, gml, mod[l], w, final, lat_tiles)
        else:
            w.update(wp=(wp, i), ps=(ps, i))
            xc = _ffn_pool(xc, mod[l], w, final, None)
            xl = _ffn_pool(xl, mod[l], w, final, lat_tiles)

    return xc.reshape(batch, seq, D), xl.reshape(dec_batch, dec_seq, D), new_ckv, new_kpe
```

```python
import functools
import math

import numpy as np
import jax
import jax.numpy as jnp
from jax import lax
from jax.experimental import pallas as pl
from jax.experimental.pallas import tpu as pltpu

F32 = jnp.float32
BF16 = jnp.bfloat16

D = 1024
DEPTH = 4
N_MOD = 6
HEADS = 8
NOPE = 64
ROPE = 32
VDIM = 64
Q_RANK = 384
KV_RANK = 256
GRID_W = 64
ROPE_BASE = 10000.0
ATTN_SCALE = (NOPE + ROPE) ** -0.5
GM_W = 512
GM_GROUPS = 8
GM_CHUNK = 128
POOL_WINDOWS = (2, 4, 8, 16)
POOL_GD = D // len(POOL_WINDOWS)
D_FF = 4 * D
EPS = 1e-6
CTX_SEQ_LEN = 256

LANES = 128
HEAD_PAD = LANES
QK_W = HEADS * HEAD_PAD
V_ROWS = VDIM + 16
VT_ROWS = HEADS * V_ROWS
MOD_ROWS = 8
HALO = 8
SEG = 256

C_QA = 0
C_KVA = C_QA + Q_RANK
C_U = C_KVA + KV_RANK
C_V = C_U + GM_W
C_KP = C_V + GM_W
C_KPS = C_KP + HEAD_PAD
C_KO = C_KPS + HEAD_PAD
IN_W = C_KO + HEAD_PAD

TM = 1024
TM_F = 1024
TF = 2048
TQ = 256
Q_TILES = 4
MAX_KEY_BLK = 2048
VMEM_LIMIT = 56 * 1024 * 1024


def _rms(x, g):
    return x * lax.rsqrt(jnp.mean(x * x, axis=-1, keepdims=True) + EPS) * g


def _const_spec(shape):
    nd = len(shape)
    return pl.BlockSpec(shape, lambda *_: (0,) * nd)


def _weights(w, names):
    arrs, specs = [], []
    for name in names:
        entry = w[name]
        if isinstance(entry, tuple):
            arr, layer = entry
            shape = arr.shape[1:]
            specs.append(pl.BlockSpec((None,) + shape, lambda *_, layer=layer, nd=len(shape): (layer,) + (0,) * nd,
                                      pipeline_mode=pl.Buffered(1)))
        else:
            arr = entry
            specs.append(pl.BlockSpec(arr.shape, lambda *_, nd=arr.ndim: (0,) * nd, pipeline_mode=pl.Buffered(1)))
        arrs.append(arr)
    return arrs, specs


def _params(n_axes, flags=None):
    return pltpu.CompilerParams(dimension_semantics=("arbitrary",) * n_axes, vmem_limit_bytes=VMEM_LIMIT,
                                flags=flags)


def _mod_kernel(cond_ref, w_ref, b_ref, o_ref):
    c = cond_ref[...]
    s = (c / (1.0 + jnp.exp(-c))).astype(BF16)
    w = w_ref[...].astype(BF16)
    o_ref[...] = jnp.dot(s, w, preferred_element_type=F32) + b_ref[...]


def _modulation(cond8, w_mod, b_mod):
    tn = 1536
    nw = N_MOD * D
    return pl.pallas_call(
        _mod_kernel,
        grid=(DEPTH, nw // tn),
        in_specs=[
            pl.BlockSpec((MOD_ROWS, D), lambda l, j: (0, 0)),
            pl.BlockSpec((None, D, tn), lambda l, j: (l, 0, j)),
            pl.BlockSpec((None, 1, tn), lambda l, j: (l, 0, j)),
        ],
        out_specs=pl.BlockSpec((None, MOD_ROWS, tn), lambda l, j: (l, 0, j)),
        out_shape=jax.ShapeDtypeStruct((DEPTH, MOD_ROWS, nw), F32),
        compiler_params=_params(2),
        name="modulation",
    )(cond8, w_mod, b_mod.reshape(DEPTH, 1, nw))


def _mod_spec(tiles_per_batch):
    if tiles_per_batch is None:
        return pl.BlockSpec((None, 1, N_MOD * D), lambda i: (0, 0, 0))
    return pl.BlockSpec((None, 1, N_MOD * D), lambda i: (1 + i // tiles_per_batch, 0, 0))


def _max_key_norm2(keys):
    kf = keys.astype(F32)
    ksq = kf * kf
    rows = []
    for h in range(HEADS):
        norm2 = jnp.sum(ksq[:, h * HEAD_PAD:(h + 1) * HEAD_PAD], axis=-1, keepdims=True)
        rows.append(jnp.broadcast_to(jnp.max(norm2, axis=0, keepdims=True), (1, LANES)))
    return jnp.concatenate(rows, axis=0)


def _in_kernel(x_ref, mod_ref, n1g_ref, wcat_ref, qag_ref, kvag_ref, wq_ref, wkb_ref, wvt_ref, aug_ref,
               vg_ref, ws_ref, bs_ref, tq_ref, ck_ref, sk_ref, *rest, is_ctx, layer, n_prev):
    prev_refs, out_refs = rest[:n_prev], rest[n_prev:]
    q_ref, k_ref, vt_ref, gm_ref, kn2_ref = out_refs[:5]
    if is_ctx:
        ckv_ref, kpe_ref = out_refs[5:]
    tm = x_ref.shape[0]
    x = x_ref[...]
    sh1 = mod_ref[:, 0 * D:1 * D]
    sc1 = mod_ref[:, 1 * D:2 * D]
    h = (_rms(x, n1g_ref[...]) * (1.0 + sc1) + sh1).astype(BF16)
    z = jnp.dot(h, wcat_ref[...], preferred_element_type=F32)

    qn = _rms(z[:, C_QA:C_QA + Q_RANK], qag_ref[...]).astype(BF16)
    qa = jnp.dot(qn, wq_ref[...], preferred_element_type=F32)
    tq8 = jnp.concatenate([tq_ref[...]] * HEADS, axis=-1)
    q_ref[...] = (qa * tq8 * (ATTN_SCALE * math.log2(math.e))).astype(BF16)

    ckv = _rms(z[:, C_KVA:C_KVA + KV_RANK], kvag_ref[...])
    kp_rot = z[:, C_KP:C_KP + HEAD_PAD] * ck_ref[...] + z[:, C_KPS:C_KPS + HEAD_PAD] * sk_ref[...]
    ckv_b = ckv.astype(BF16)
    kn = jnp.dot(ckv_b, wkb_ref[...], preferred_element_type=F32)
    keys = (kn + jnp.concatenate([kp_rot] * HEADS, axis=-1)).astype(BF16)
    k_ref[...] = keys
    kn2_ref[...] = _max_key_norm2(keys)
    vt = lax.dot_general(wvt_ref[...], ckv_b, (((1,), (1,)), ((), ())), preferred_element_type=F32)
    vt_ref[...] = (vt + aug_ref[...]).astype(BF16)
    if is_ctx:
        seqs, n_ab = ckv_ref.shape[0], ckv_ref.shape[1]
        for slot in range(n_ab):
            if slot == layer:
                ckv_ref[:, slot] = ckv.reshape(seqs, CTX_SEQ_LEN, KV_RANK)
                kpe_ref[:, slot] = z[:, C_KO:C_KO + ROPE].reshape(seqs, CTX_SEQ_LEN, ROPE)
            elif prev_refs:
                ckv_ref[:, slot] = prev_refs[0][:, slot]
                kpe_ref[:, slot] = prev_refs[1][:, slot]
            else:
                ckv_ref[:, slot] = jnp.zeros((seqs, CTX_SEQ_LEN, KV_RANK), F32)
                kpe_ref[:, slot] = jnp.zeros((seqs, CTX_SEQ_LEN, ROPE), F32)

    u = jax.nn.gelu(z[:, C_U:C_U + GM_W])
    vv = jax.nn.gelu(z[:, C_V:C_V + GM_W])
    vn = _rms(vv, vg_ref[...])
    lane = lax.broadcasted_iota(jnp.int32, (GM_CHUNK, LANES), 1)
    lo = lane < GM_W // GM_GROUPS
    for c in range(tm // GM_CHUNK):
        rows = slice(c * GM_CHUNK, (c + 1) * GM_CHUNK)
        for j in range(GM_W // LANES):
            cols = slice(j * LANES, (j + 1) * LANES)
            blk = vn[rows, cols]
            rhs = jnp.concatenate([jnp.where(lo, blk, 0.0), jnp.where(lo, 0.0, blk)], axis=0).astype(BF16)
            s = jnp.dot(ws_ref[j], rhs, preferred_element_type=F32) + bs_ref[:, cols]
            gm_ref[rows, cols] = (u[rows, cols] * s).astype(BF16)


def _in_proj(x, mod_l, w, tabs, *, layer, ckv_prev=None, kpe_prev=None, lat_shape=None):
    n = x.shape[0]
    is_ctx = lat_shape is None
    row = lambda i: (i, 0)
    id_blk = tabs[0].shape[0] // TM - 1
    if is_ctx:
        tpb = None
        tab_idx = lambda i: (id_blk, 0)
        vt_shape, vt_spec = (VT_ROWS, n), pl.BlockSpec((VT_ROWS, TM), lambda i: (0, i))
    else:
        dec_batch, dec_seq = lat_shape
        tpb = dec_seq // TM
        tab_idx = lambda i: (i % tpb, 0)
        vt_shape = (dec_batch, VT_ROWS, dec_seq)
        vt_spec = pl.BlockSpec((None, VT_ROWS, TM), lambda i: (i // tpb, 0, i % tpb))
    w_arrs, w_specs = _weights(w, ["n1g", "wcat", "qag", "kvag", "wq", "wkb", "wvt", "aug", "vg", "ws", "bs"])
    in_specs = [pl.BlockSpec((TM, D), row), _mod_spec(tpb)] + w_specs + [pl.BlockSpec((TM, HEAD_PAD), tab_idx)] * 3
    args = [x, mod_l, *w_arrs, *tabs]
    out_shape = [jax.ShapeDtypeStruct((n, QK_W), BF16), jax.ShapeDtypeStruct((n, QK_W), BF16),
                 jax.ShapeDtypeStruct(vt_shape, BF16), jax.ShapeDtypeStruct((n, GM_W), BF16),
                 jax.ShapeDtypeStruct((n // TM, HEADS, LANES), F32)]
    out_specs = [pl.BlockSpec((TM, QK_W), row), pl.BlockSpec((TM, QK_W), row), vt_spec,
                 pl.BlockSpec((TM, GM_W), row), pl.BlockSpec((None, HEADS, LANES), lambda i: (i, 0, 0))]
    aliases = {}
    n_prev = 0
    if is_ctx:
        seqs = TM // CTX_SEQ_LEN
        batch = n // CTX_SEQ_LEN
        n_ab = DEPTH // 2
        cache_specs = [pl.BlockSpec((seqs, n_ab, CTX_SEQ_LEN, KV_RANK), lambda i: (i, 0, 0, 0)),
                       pl.BlockSpec((seqs, n_ab, CTX_SEQ_LEN, ROPE), lambda i: (i, 0, 0, 0))]
        out_shape += [jax.ShapeDtypeStruct((batch, n_ab, CTX_SEQ_LEN, KV_RANK), F32),
                      jax.ShapeDtypeStruct((batch, n_ab, CTX_SEQ_LEN, ROPE), F32)]
        out_specs += cache_specs
        if ckv_prev is not None:
            in_specs += cache_specs
            args += [ckv_prev, kpe_prev]
            aliases = {len(args) - 2: 5, len(args) - 1: 6}
            n_prev = 2

    return pl.pallas_call(
        functools.partial(_in_kernel, is_ctx=is_ctx, layer=layer, n_prev=n_prev),
        grid=(n // TM,),
        in_specs=in_specs,
        out_specs=out_specs,
        out_shape=out_shape,
        input_output_aliases=aliases,
        compiler_params=_params(1),
        name="in_proj_ctx" if is_ctx else "in_proj_lat",
    )(*args)


def _cache_kernel(ckv_ref, kpe_ref, wkb_ref, wvt_ref, aug_ref, place_ref, k_ref, vt_ref, kn2_ref):
    ckv_b = ckv_ref[...].astype(BF16)
    kn = jnp.dot(ckv_b, wkb_ref[...], preferred_element_type=F32)
    kp = jnp.dot(kpe_ref[...].astype(BF16), place_ref[...], preferred_element_type=F32)
    keys = (kn + jnp.concatenate([kp] * HEADS, axis=-1)).astype(BF16)
    k_ref[...] = keys
    kn2_ref[...] = _max_key_norm2(keys)
    vt = lax.dot_general(wvt_ref[...], ckv_b, (((1,), (1,)), ((), ())), preferred_element_type=F32)
    vt_ref[...] = (vt + aug_ref[...]).astype(BF16)


def _cache_keys(cache_ckv, cache_kpe, w, place, *, layer):
    dec_batch, _, past, _ = cache_ckv.shape
    w_arrs, w_specs = _weights(dict(w, place=place), ["wkb", "wvt", "aug", "place"])
    return pl.pallas_call(
        _cache_kernel,
        grid=(dec_batch,),
        in_specs=[
            pl.BlockSpec((None, None, past, KV_RANK), lambda b: (b, layer, 0, 0)),
            pl.BlockSpec((None, None, past, ROPE), lambda b: (b, layer, 0, 0)),
            *w_specs,
        ],
        out_specs=[pl.BlockSpec((None, past, QK_W), lambda b: (b, 0, 0)),
                   pl.BlockSpec((None, VT_ROWS, past), lambda b: (b, 0, 0)),
                   pl.BlockSpec((None, HEADS, LANES), lambda b: (b, 0, 0))],
        out_shape=(jax.ShapeDtypeStruct((dec_batch, past, QK_W), BF16),
                   jax.ShapeDtypeStruct((dec_batch, VT_ROWS, past), BF16),
                   jax.ShapeDtypeStruct((dec_batch, HEADS, LANES), F32)),
        compiler_params=_params(1),
        name="cache_keys",
    )(cache_ckv, cache_kpe, *w_arrs)


SAFE_DENOM = 2.0 ** -100
BOUND_SLACK = 1.02


def _attn_kernel(q_ref, *refs, n_src, segment):
    srcs = [(refs[2 * s], refs[2 * s + 1]) for s in range(n_src)]
    kmax_ref, o_ref, st_ref, p_ref = refs[2 * n_src:]

    def key_blocks(qt):
        blocks = []
        base = 0
        for k_ref, vt_ref in srcs:
            l_src = k_ref.shape[0] if segment is None else segment
            for start in range(0, l_src, MAX_KEY_BLK):
                size = min(MAX_KEY_BLK, l_src - start)
                if segment is None:
                    src_rows = slice(start, start + size)
                elif isinstance(qt, int):
                    src_rows = slice(qt * segment + start, qt * segment + start + size)
                else:
                    src_rows = pl.ds(pl.multiple_of(qt * segment + start, 2 * LANES), size)
                blocks.append((k_ref, vt_ref, src_rows, slice(base + start, base + start + size)))
            base += l_src
        return blocks

    n_qt = q_ref.shape[0] // TQ
    items = [(qt, h) for qt in range(n_qt) for h in range(HEADS)]
    n_items = len(items)
    ones = jnp.ones((8, HEAD_PAD), BF16)

    def finish(ot, qt, h, pair):
        pair[h] = ot[0:VDIM] * (1.0 / ot[VDIM:VDIM + 1])
        if h % 2 == 1:
            both = jnp.concatenate([pair.pop(h - 1), pair.pop(h)], axis=0)
            o_ref[qt * TQ:(qt + 1) * TQ, (h // 2) * LANES:(h // 2 + 1) * LANES] = both.T.astype(BF16)

    pair = {}
    bad = None
    for step in range(n_items + 1):
        ia, ic = step, step - 1
        ot = None
        if ia < n_items:
            qt, h = items[ia]
            cols = slice(h * HEAD_PAD, (h + 1) * HEAD_PAD)
            qh = q_ref[qt * TQ:(qt + 1) * TQ, cols]
            qf = qh.astype(F32)
            qn2 = lax.dot_general(ones, (qf * qf).astype(BF16), (((1,), (1,)), ((), ())),
                                  preferred_element_type=F32)[0:1]
            km2 = kmax_ref[h:h + 1, :]
            shift = jnp.sqrt(qn2 * jnp.concatenate([km2] * (TQ // LANES), axis=-1)) * BOUND_SLACK
        for j in range(len(key_blocks(0))):
            if ia < n_items:
                k_ref, _, src_rows, rows = key_blocks(qt)[j]
                st = lax.dot_general(k_ref[src_rows, cols], qh, (((1,), (1,)), ((), ())),
                                     preferred_element_type=F32)
                p_ref[ia % 2, rows, :] = jnp.exp2(st - shift).astype(BF16)
            if 0 <= ic < n_items:
                qc, hc = items[ic]
                _, vt_ref, src_rows, rows = key_blocks(qc)[j]
                part = jnp.dot(vt_ref[hc * V_ROWS:(hc + 1) * V_ROWS, src_rows], p_ref[ic % 2, rows, :],
                               preferred_element_type=F32)
                ot = part if ot is None else ot + part
        if 0 <= ic < n_items:
            unsafe = jnp.where(ot[VDIM:VDIM + 1] >= SAFE_DENOM, 0.0, 1.0)
            bad = unsafe if bad is None else jnp.maximum(bad, unsafe)
            finish(ot, *items[ic], pair)

    @pl.when(jnp.max(bad) > 0.0)
    def _():
        def redo(i, carry):
            qt = i // (HEADS // 2)
            hp = i % (HEADS // 2)
            q_rows = pl.ds(pl.multiple_of(qt * TQ, TQ), TQ)
            blocks = key_blocks(qt)
            outs = []
            for e in range(2):
                h = 2 * hp + e
                cols = pl.ds(pl.multiple_of(h * HEAD_PAD, HEAD_PAD), HEAD_PAD)
                qh = q_ref[q_rows, cols]
                for k_ref, _, src_rows, rows in blocks:
                    st_ref[0, rows, :] = lax.dot_general(k_ref[src_rows, cols], qh, (((1,), (1,)), ((), ())),
                                                         preferred_element_type=F32)
                m = jnp.max(st_ref[0], axis=0, keepdims=True)
                p_ref[0] = jnp.exp2(st_ref[0] - m).astype(BF16)
                ot = None
                v_rows = pl.ds(pl.multiple_of(h * V_ROWS, 16), V_ROWS)
                for _, vt_ref, src_rows, rows in blocks:
                    part = jnp.dot(vt_ref[v_rows, src_rows], p_ref[0, rows, :], preferred_element_type=F32)
                    ot = part if ot is None else ot + part
                outs.append(ot[0:VDIM] * (1.0 / ot[VDIM:VDIM + 1]))
            o_ref[q_rows, pl.ds(pl.multiple_of(hp * LANES, LANES), LANES)] = (
                jnp.concatenate(outs, axis=0).T.astype(BF16))
            return carry

        lax.fori_loop(0, n_qt * (HEADS // 2), redo, 0)


def _attention(q, sources, kmax2, *, batch, t_len, segment=None):
    tq = min(t_len, Q_TILES * TQ)
    nq = t_len // tq
    in_specs = [pl.BlockSpec((tq, QK_W), lambda b, qi: (b * nq + qi, 0))]
    args = [q]
    l_total = 0
    for k, vt in sources:
        if segment is None:
            l_len = k.shape[1]
            l_total += l_len
            in_specs += [pl.BlockSpec((None, l_len, QK_W), lambda b, qi: (b, 0, 0)),
                         pl.BlockSpec((None, VT_ROWS, l_len), lambda b, qi: (b, 0, 0))]
        else:
            assert segment == TQ and nq == 1
            l_total += segment
            in_specs += [pl.BlockSpec((tq, QK_W), lambda b, qi: (b, 0)),
                         pl.BlockSpec((VT_ROWS, tq), lambda b, qi: (0, b))]
        args += [k, vt]
    in_specs.append(pl.BlockSpec((None, HEADS, LANES), lambda b, qi: (b, 0, 0)))
    args.append(kmax2)
    return pl.pallas_call(
        functools.partial(_attn_kernel, n_src=len(sources), segment=segment),
        grid=(batch, nq),
        scratch_shapes=[pltpu.VMEM((1, l_total, TQ), F32), pltpu.VMEM((2, l_total, TQ), BF16)],
        in_specs=in_specs,
        out_specs=pl.BlockSpec((tq, HEADS * VDIM), lambda b, qi: (b * nq + qi, 0)),
        out_shape=jax.ShapeDtypeStruct((batch * t_len, HEADS * VDIM), BF16),
        compiler_params=_params(2),
        name="attention",
    )(*args)


def _mlp_tail(x1, mod_ref, n2g_ref, w1_ref, w2_ref, fg_ref, o_ref, final):
    sh2 = mod_ref[:, 3 * D:4 * D]
    sc2 = mod_ref[:, 4 * D:5 * D]
    g2 = mod_ref[:, 5 * D:6 * D]
    h = (_rms(x1, n2g_ref[...]) * (1.0 + sc2) + sh2).astype(BF16)
    acc = None
    for c in range(D_FF // TF):
        a = jnp.maximum(jnp.dot(h, w1_ref[:, c * TF:(c + 1) * TF], preferred_element_type=F32), 0.0)
        part = jnp.dot((a * a).astype(BF16), w2_ref[c * TF:(c + 1) * TF, :], preferred_element_type=F32)
        acc = part if acc is None else acc + part
    x2 = x1 + g2 * acc
    if final:
        x2 = _rms(x2, fg_ref[...])
    o_ref[...] = x2


def _ffn_ab_kernel(x_ref, attn_ref, gm_ref, mod_ref, wo_ref, n2g_ref, w1_ref, w2_ref, fg_ref, o_ref, *, final):
    g1 = mod_ref[:, 2 * D:3 * D]
    mix = jnp.dot(jnp.concatenate([attn_ref[...], gm_ref[...]], axis=-1), wo_ref[...],
                  preferred_element_type=F32)
    x1 = x_ref[...] + g1 * mix
    _mlp_tail(x1, mod_ref, n2g_ref, w1_ref, w2_ref, fg_ref, o_ref, final)


def _ffn_pool_kernel(x_ref, xp_ref, xn_ref, mod_ref, n1g_ref, band_ref, wp_ref, ps_ref, n2g_ref, w1_ref, w2_ref,
                     fg_ref, o_ref, *, final, tiles_per_seq):
    sh1 = mod_ref[:, 0 * D:1 * D]
    sc1 = mod_ref[:, 1 * D:2 * D]
    g1 = mod_ref[:, 2 * D:3 * D]
    n1g = n1g_ref[...]
    ada = lambda v: _rms(v, n1g) * (1.0 + sc1) + sh1
    x = x_ref[...]
    ht = ada(x)
    zero_halo = jnp.zeros((HALO, D), F32)
    if tiles_per_seq is None:
        seq_len, pos0 = SEG, 0
    else:
        t = pl.program_id(0) % tiles_per_seq
        seq_len, pos0 = tiles_per_seq * TM_F, t * TM_F
        tile_prev = jnp.where(t != 0, ada(xp_ref[...]), 0.0)
        tile_next = jnp.where(t != tiles_per_seq - 1, ada(xn_ref[...]), 0.0)
    n_seg = TM_F // SEG
    seg_rows = []
    for s in range(n_seg):
        seg = ht[s * SEG:(s + 1) * SEG]
        if tiles_per_seq is None:
            prev, nxt, base = zero_halo, zero_halo, 0
        else:
            prev = tile_prev if s == 0 else ht[s * SEG - HALO:s * SEG]
            nxt = tile_next if s == n_seg - 1 else ht[(s + 1) * SEG:(s + 1) * SEG + HALO]
            base = pos0 + s * SEG
        hs = jnp.concatenate([prev, seg, nxt], axis=0)
        hi = hs.astype(BF16)
        lo = (hs - hi.astype(F32)).astype(BF16)
        pos = base + lax.broadcasted_iota(jnp.int32, (SEG, POOL_GD), 0)
        parts = []
        for gi, w in enumerate(POOL_WINDOWS):
            half = w // 2
            cols = slice(gi * POOL_GD, (gi + 1) * POOL_GD)
            top = (jnp.dot(band_ref[gi, 0], hi[0:SEG, cols], preferred_element_type=F32)
                   + jnp.dot(band_ref[gi, 0], lo[0:SEG, cols], preferred_element_type=F32))
            bot = (jnp.dot(band_ref[gi, 1], hi[2 * HALO:, cols], preferred_element_type=F32)
                   + jnp.dot(band_ref[gi, 1], lo[2 * HALO:, cols], preferred_element_type=F32))
            tot = jnp.concatenate([top, bot], axis=0)
            cnt = (jnp.minimum(pos + half, seq_len) - jnp.maximum(pos - half, 0)).astype(F32)
            diff = (tot / cnt - seg[:, cols]).astype(BF16)
            parts.append(jnp.dot(diff, wp_ref[gi], preferred_element_type=F32))
        seg_rows.append(jnp.concatenate(parts, axis=-1))
    mix = jnp.concatenate(seg_rows, axis=0) * ps_ref[...]
    x1 = x + g1 * mix
    _mlp_tail(x1, mod_ref, n2g_ref, w1_ref, w2_ref, fg_ref, o_ref, final)


def _pool_bands():
    half_rows = SEG // 2
    t = np.arange(half_rows)[:, None]
    u = np.arange(SEG)[None, :]
    bands = np.zeros((len(POOL_WINDOWS), 2, half_rows, SEG), np.float32)
    for gi, w in enumerate(POOL_WINDOWS):
        half = w // 2
        d_top = (u - HALO) - t
        d_bot = (u + HALO) - (half_rows + t)
        bands[gi, 0] = (d_top >= -half) & (d_top <= half - 1)
        bands[gi, 1] = (d_bot >= -half) & (d_bot <= half - 1)
    return jnp.asarray(bands, BF16)


def _ffn_ab(x, attn, gm, mod_l, w, final, tiles_per_batch):
    n = x.shape[0]
    row = lambda i: (i, 0)
    w_arrs, w_specs = _weights(w, ["wo", "n2g", "w1", "w2", "fg"])
    return pl.pallas_call(
        functools.partial(_ffn_ab_kernel, final=final),
        grid=(n // TM_F,),
        in_specs=[
            pl.BlockSpec((TM_F, D), row),
            pl.BlockSpec((TM_F, HEADS * VDIM), row),
            pl.BlockSpec((TM_F, GM_W), row),
            _mod_spec(tiles_per_batch),
        ] + w_specs,
        out_specs=pl.BlockSpec((TM_F, D), row),
        out_shape=jax.ShapeDtypeStruct((n, D), F32),
        compiler_params=_params(1),
        name="ffn_ab",
    )(x, attn, gm, mod_l, *w_arrs)


def _ffn_pool(x, mod_l, w, final, tiles_per_batch):
    n = x.shape[0]
    row = lambda i: (i, 0)
    hb = TM_F // HALO
    w_arrs, w_specs = _weights(w, ["n1g", "bands", "wp", "ps", "n2g", "w1", "w2", "fg"])
    return pl.pallas_call(
        functools.partial(_ffn_pool_kernel, final=final, tiles_per_seq=tiles_per_batch),
        grid=(n // TM_F,),
        in_specs=[
            pl.BlockSpec((TM_F, D), row),
            pl.BlockSpec((HALO, D), lambda i: (jnp.maximum(i * hb - 1, 0), 0)),
            pl.BlockSpec((HALO, D), lambda i: (jnp.minimum((i + 1) * hb, n // HALO - 1), 0)),
            _mod_spec(tiles_per_batch),
        ] + w_specs,
        out_specs=pl.BlockSpec((TM_F, D), row),
        out_shape=jax.ShapeDtypeStruct((n, D), F32),
        compiler_params=_params(1),
        name="ffn_pool",
    )(x, x, x, mod_l, *w_arrs)


_SWAP = np.arange(ROPE).reshape(2, 2, ROPE // 4)[:, ::-1, :].reshape(ROPE)


def _rope_tables(t_len):
    rows = t_len // GRID_W
    row = jnp.repeat(jnp.arange(rows), GRID_W).astype(F32)
    col = jnp.tile(jnp.arange(GRID_W), rows).astype(F32)
    axis_dim = ROPE // 2
    inv = ROPE_BASE ** (-jnp.arange(0, axis_dim, 2, dtype=F32) / axis_dim)
    ang = jnp.stack([row[:, None] * inv, col[:, None] * inv], axis=1)
    cos = jnp.cos(ang)
    sin = jnp.sin(ang)
    c32 = jnp.stack([cos, cos], axis=2).reshape(t_len, ROPE)
    s32 = jnp.stack([-sin, sin], axis=2).reshape(t_len, ROPE)
    one = lambda r, c: jnp.ones((r, c), F32)
    zero = lambda r, c: jnp.zeros((r, c), F32)
    tq = jnp.concatenate([jnp.concatenate([one(t_len, NOPE), c32, s32], -1),
                          jnp.concatenate([one(TM, NOPE + ROPE), zero(TM, ROPE)], -1)], 0)
    ck = jnp.concatenate([jnp.concatenate([zero(t_len, NOPE), c32, c32], -1),
                          jnp.concatenate([zero(TM, NOPE), one(TM, 2 * ROPE)], -1)], 0)
    sk = jnp.concatenate([jnp.concatenate([zero(t_len, NOPE), s32, s32], -1), zero(TM, HEAD_PAD)], 0)
    return tq, ck, sk


def _prep_even_weights(w_in_ab, q_a_g, kv_a_g, w_q_b, w_kv_b, gmlp_v_g, w_spatial, b_spatial, w_out_ab):
    n_ab = w_in_ab.shape[0]
    kpe_off = Q_RANK + KV_RANK
    kp_w = w_in_ab[:, :, kpe_off:kpe_off + ROPE]
    kp_sw = kp_w[:, :, _SWAP]
    z = lambda c: jnp.zeros((n_ab, D, c), F32)
    wcat = jnp.concatenate(
        [w_in_ab[:, :, :kpe_off], w_in_ab[:, :, kpe_off + ROPE:],
         z(NOPE), kp_w, kp_w, z(NOPE), kp_sw, kp_sw, kp_w, z(HEAD_PAD - ROPE)], axis=2).astype(BF16)
    wq4 = w_q_b.reshape(n_ab, Q_RANK, HEADS, NOPE + ROPE)
    wq = jnp.concatenate([wq4, wq4[..., NOPE:][..., _SWAP]], axis=-1).reshape(n_ab, Q_RANK, QK_W).astype(BF16)
    wkv4 = w_kv_b.reshape(n_ab, KV_RANK, HEADS, NOPE + VDIM)
    wkb = jnp.pad(wkv4[..., :NOPE], ((0, 0), (0, 0), (0, 0), (0, HEAD_PAD - NOPE))).reshape(
        n_ab, KV_RANK, QK_W).astype(BF16)
    wvt = jnp.pad(wkv4[..., NOPE:].transpose(0, 2, 3, 1), ((0, 0), (0, 0), (0, V_ROWS - VDIM), (0, 0))).reshape(
        n_ab, VT_ROWS, KV_RANK).astype(BF16)
    ws = w_spatial.reshape(n_ab, GM_GROUPS // 2, 2, GM_CHUNK, GM_CHUNK).transpose(0, 1, 3, 2, 4).reshape(
        n_ab, GM_GROUPS // 2, GM_CHUNK, 2 * GM_CHUNK).astype(BF16)
    bs = jnp.repeat(b_spatial.transpose(0, 2, 1), GM_W // GM_GROUPS, axis=2)
    aug = np.zeros((VT_ROWS, 1), np.float32)
    aug[VDIM::V_ROWS] = 1.0
    return dict(wcat=wcat, wq=wq, wkb=wkb, wvt=wvt, ws=ws, bs=bs, aug=jnp.asarray(aug),
                qag=q_a_g[:, None], kvag=kv_a_g[:, None], vg=gmlp_v_g[:, None], wo=w_out_ab.astype(BF16))


def kernel(x_prompt, x_sample, cache_ckv, cache_kpe, c, c_ctx, w_mod, b_mod, norm1_g, norm2_g, w_in_ab, q_a_g,
           kv_a_g, w_q_b, w_kv_b, gmlp_v_g, w_spatial, b_spatial, w_out_ab, w_pool, pool_scale, w_ff1, w_ff2,
           final_g):
    batch, seq, _ = x_prompt.shape
    dec_batch, dec_seq, _ = x_sample.shape
    past = cache_ckv.shape[2]
    assert seq == CTX_SEQ_LEN and dec_seq % TM_F == 0 and dec_seq % TM == 0 and past % (2 * LANES) == 0
    lat_tiles = dec_seq // TM_F

    xc = x_prompt.reshape(batch * seq, D)
    xl = x_sample.reshape(dec_batch * dec_seq, D)
    cond8 = jnp.concatenate([c_ctx[None], c, jnp.zeros((MOD_ROWS - 1 - dec_batch, D), F32)], axis=0)
    mod = _modulation(cond8, w_mod, b_mod).reshape(DEPTH, MOD_ROWS, 1, N_MOD * D)
    tabs = _rope_tables(dec_seq)
    bands = _pool_bands()
    place = np.zeros((ROPE, HEAD_PAD), np.float32)
    place[np.arange(ROPE), NOPE + np.arange(ROPE)] = 1.0
    place[np.arange(ROPE), NOPE + ROPE + np.arange(ROPE)] = 1.0
    place = jnp.asarray(place, BF16)

    ev = _prep_even_weights(w_in_ab, q_a_g, kv_a_g, w_q_b, w_kv_b, gmlp_v_g, w_spatial, b_spatial, w_out_ab)
    w1 = w_ff1.astype(BF16)
    w2 = w_ff2.astype(BF16)
    wp = w_pool.astype(BF16)
    n1g = norm1_g[:, None]
    n2g = norm2_g[:, None]
    ps = pool_scale[:, None]

    new_ckv = new_kpe = None
    for l in range(DEPTH):
        final = l == DEPTH - 1
        i = l // 2
        w = dict(n1g=(n1g, l), n2g=(n2g, l), w1=(w1, l), w2=(w2, l), fg=final_g[None], bands=bands)
        if l % 2 == 0:
            w.update({name: (val, i) for name, val in ev.items() if name != "aug"}, aug=ev["aug"])
            qc, kc, vtc, gmc, kn2_ctx, new_ckv, new_kpe = _in_proj(
                xc, mod[l], w, tabs, layer=i, ckv_prev=new_ckv, kpe_prev=new_kpe)
            ql, kl, vtl, gml, kn2_l = _in_proj(xl, mod[l], w, tabs, layer=i, lat_shape=(dec_batch, dec_seq))
            k_cache, vt_cache, kn2_c = _cache_keys(cache_ckv, cache_kpe, w, place, layer=i)
            kmax2 = jnp.maximum(kn2_l.reshape(dec_batch, -1, HEADS, LANES).max(axis=1), kn2_c)
            attn_c = _attention(qc, [(kc, vtc)], kn2_ctx, batch=batch * seq // TM, t_len=TM, segment=seq)
            attn_l = _attention(ql, [(k_cache, vt_cache), (kl.reshape(dec_batch, dec_seq, QK_W), vtl)], kmax2,
                                batch=dec_batch, t_len=dec_seq)
            xc = _ffn_ab(xc, attn_c, gmc, mod[l], w, final, None)
            xl = _ffn_ab(xl, attn_l, gml, mod[l], w, final, lat_tiles)
        else:
            w.update(wp=(wp, i), ps=(ps, i))
            xc = _ffn_pool(xc, mod[l], w, final, None)
            xl = _ffn_pool(xl, mod[l], w, final, lat_tiles)

    return xc.reshape(batch, seq, D), xl.reshape(dec_batch, dec_seq, D), new_ckv, new_kpe
```
